```python
import jax, jax.numpy as jnp
from jax import lax
import numpy as np

D_MODEL = 1024
BATCH = 8
SEQ = 8192
DEPTH = 2
DEC_BATCH = 16
DEC_SEQ = 64
PAST_LEN = 1024

CHUNK = 64
Q_BLOCK = 128
PLE_DIM = 256
MIX_WIDTH = D_MODEL
SB_WIDTH = MIX_WIDTH // 2
SB_HEAD_DIM = 64
SB_HEADS = SB_WIDTH // SB_HEAD_DIM
RG_WIDTH = MIX_WIDTH - SB_WIDTH
RG_BLOCKS = 8
RG_BLOCK_DIM = RG_WIDTH // RG_BLOCKS
RG_CONV_W = 4
RG_C = 8.0
CM_CONV_W = 31
D_FF = 2816
N_AB = (DEPTH + 1) // 2
N_CM = DEPTH // 2
AB_IN_WIDTH = 3 * SB_WIDTH + 2 * RG_WIDTH
NORM_EPS = 1e-6

kernel_name = 'stickbreak_rglru_conformer_stream_step'


def rms_norm(x, g):
    xf = x.astype(jnp.float32)
    y = xf * lax.rsqrt(jnp.mean(xf * xf, axis=-1, keepdims=True) + NORM_EPS)
    return (y * g.astype(jnp.float32)).astype(x.dtype)


def layer_norm(x, g, b):
    xf = x.astype(jnp.float32)
    xc = xf - jnp.mean(xf, axis=-1, keepdims=True)
    y = xc * lax.rsqrt(jnp.mean(xc * xc, axis=-1, keepdims=True) + NORM_EPS)
    return (y * g.astype(jnp.float32) + b.astype(jnp.float32)).astype(x.dtype)


def swiglu(x, wg, wu, wd):
    return (jax.nn.silu(x @ wg) * (x @ wu)) @ wd


def causal_dwconv(x, state, w, b):
    xp = jnp.concatenate([state.astype(x.dtype), x], axis=1)
    y = lax.conv_general_dilated(xp, w[:, None, :].astype(x.dtype), window_strides=(1,), padding='VALID',
                                 dimension_numbers=('NWC', 'WIO', 'NWC'), feature_group_count=x.shape[-1])
    return y + b, xp[:, xp.shape[1] - (w.shape[0] - 1):]


def stick_breaking_block(q, k, v, q_pos, k_pos):
    z = jnp.einsum('bhqd,bhkd->bhqk', q, k).astype(jnp.float32) * (SB_HEAD_DIM ** -0.5)
    mask = k_pos[None, :] < q_pos[:, None]
    log_keep = jnp.where(mask, jax.nn.log_sigmoid(-z), 0.0)
    later = lax.cumsum(log_keep, axis=3, reverse=True) - log_keep
    a = jnp.where(mask, jnp.exp(jax.nn.log_sigmoid(z) + later), 0.0)
    return jnp.einsum('bhqk,bhkd->bhqd', a.astype(v.dtype), v)


def rg_lru(y, h0, w_a, b_a, w_x, b_x, lam):
    bsz, t_len, _ = y.shape
    yb = y.reshape(bsz, t_len, RG_BLOCKS, RG_BLOCK_DIM)
    r = jax.nn.sigmoid(jnp.einsum('btgi,gij->btgj', yb, w_a).reshape(bsz, t_len, RG_WIDTH) + b_a)
    ig = jax.nn.sigmoid(jnp.einsum('btgi,gij->btgj', yb, w_x).reshape(bsz, t_len, RG_WIDTH) + b_x)
    log_a = (-RG_C * jax.nn.softplus(-lam.astype(jnp.float32))) * r.astype(jnp.float32)
    a = jnp.exp(log_a)
    b = jnp.sqrt(-jnp.expm1(2.0 * log_a)) * (ig * y).astype(jnp.float32)
    b = b.at[:, 0].add(a[:, 0] * h0.astype(jnp.float32))

    def combine(left, right):
        return (left[0] * right[0], right[0] * left[1] + right[1])

    _, h = lax.associative_scan(combine, (a, b), axis=1)
    return h.astype(y.dtype), h[:, -1].astype(y.dtype)


def ab_mixer(hn, w_in, w_out, conv_w, conv_b, ga_w, ga_b, gx_w, gx_b, lam, cache_k, cache_v, conv_state, h0):
    bsz, t_len, _ = hn.shape
    u = hn @ w_in
    q, k, v, xr, xg = jnp.split(u, [SB_WIDTH, 2 * SB_WIDTH, 3 * SB_WIDTH, 3 * SB_WIDTH + RG_WIDTH], axis=-1)

    def heads(t):
        return t.reshape(bsz, t_len, SB_HEADS, SB_HEAD_DIM).transpose(0, 2, 1, 3)

    q, k, v = heads(q), heads(k), heads(v)
    if cache_k is None:
        outs = []
        for q0 in range(0, t_len, Q_BLOCK):
            q1 = min(q0 + Q_BLOCK, t_len)
            outs.append(stick_breaking_block(q[:, :, q0:q1], k[:, :, :q1], v[:, :, :q1],
                                             jnp.arange(q0, q1), jnp.arange(q1)))
        sb = jnp.concatenate(outs, axis=2)
    else:
        past = cache_k.shape[2]
        k_all = jnp.concatenate([cache_k.astype(k.dtype), k], axis=2)
        v_all = jnp.concatenate([cache_v.astype(v.dtype), v], axis=2)
        sb = stick_breaking_block(q, k_all, v_all, past + jnp.arange(t_len), jnp.arange(past + t_len))
    sb = sb.transpose(0, 2, 1, 3).reshape(bsz, t_len, SB_WIDTH)
    xc, conv_new = causal_dwconv(xr, conv_state, conv_w, conv_b)
    hseq, h_last = rg_lru(xc, h0, ga_w, ga_b, gx_w, gx_b, lam)
    rg = hseq * jax.nn.gelu(xg)
    out = jnp.concatenate([sb, rg], axis=-1) @ w_out
    return out, k, v, conv_new, h_last


def conv_module(hn, w_pw1, b_pw1, dw_w, dw_b, ln_g, ln_b, w_pw2, b_pw2, conv_state):
    a, g = jnp.split(hn @ w_pw1 + b_pw1, 2, axis=-1)
    u = a * jax.nn.sigmoid(g)
    c, conv_new = causal_dwconv(u, conv_state, dw_w, dw_b)
    c = jax.nn.silu(layer_norm(c, ln_g, ln_b))
    return c @ w_pw2 + b_pw2, conv_new


def trunk(x, p, w, sb_k, sb_v, rg_conv, rg_h, cm_conv):
    bsz = x.shape[0]
    fresh = sb_k is None
    h = x
    out_k, out_v, out_rc, out_rh, out_cc = [], [], [], [], []
    for i in range(DEPTH):
        h = h + 0.5 * swiglu(rms_norm(h, w['norm_ffn'][i, 0]), w['w_ffn_gate'][i, 0], w['w_ffn_up'][i, 0], w['w_ffn_down'][i, 0])
        hn = rms_norm(h, w['norm_mix'][i])
        j = i // 2
        if i % 2 == 0:
            conv0 = jnp.zeros((bsz, RG_CONV_W - 1, RG_WIDTH), x.dtype) if fresh else rg_conv[j]
            h0 = jnp.zeros((bsz, RG_WIDTH), x.dtype) if fresh else rg_h[j]
            mix, k_new, v_new, rc_new, rh_new = ab_mixer(
                hn, w['w_ab_in'][j], w['w_ab_out'][j], w['rg_conv_w'][j], w['rg_conv_b'][j],
                w['rg_gate_a_w'][j], w['rg_gate_a_b'][j], w['rg_gate_x_w'][j], w['rg_gate_x_b'][j],
                w['rg_lambda'][j], None if fresh else sb_k[j], None if fresh else sb_v[j], conv0, h0)
            out_k.append(k_new)
            out_v.append(v_new)
            out_rc.append(rc_new)
            out_rh.append(rh_new)
        else:
            cc0 = jnp.zeros((bsz, CM_CONV_W - 1, D_MODEL), x.dtype) if fresh else cm_conv[j]
            mix, cc_new = conv_module(hn, w['cm_pw1_w'][j], w['cm_pw1_b'][j], w['cm_dw_w'][j], w['cm_dw_b'][j],
                                      w['cm_ln_g'][j], w['cm_ln_b'][j], w['cm_pw2_w'][j], w['cm_pw2_b'][j], cc0)
            out_cc.append(cc_new)
        h = h + mix
        h = h + 0.5 * swiglu(rms_norm(h, w['norm_ffn'][i, 1]), w['w_ffn_gate'][i, 1], w['w_ffn_up'][i, 1], w['w_ffn_down'][i, 1])
        gate = jax.nn.sigmoid(rms_norm(h, w['norm_ple'][i]) @ w['w_ple_gate'][i])
        h = h + gate * (p[i] @ w['w_ple_proj'][i])
    y = rms_norm(h, w['norm_final'])
    return y, jnp.stack(out_k), jnp.stack(out_v), jnp.stack(out_rc), jnp.stack(out_rh), jnp.stack(out_cc)


def setup_inputs(seed: int = 0) -> dict:
    key = jax.random.key(seed)
    ks = iter(jax.random.split(key, 48))

    def nrm(shape, scale):
        return scale * jax.random.normal(next(ks), shape, jnp.float32)

    def gain(shape):
        return 1.0 + 0.05 * jax.random.normal(next(ks), shape, jnp.float32)

    u = jax.random.uniform(next(ks), (N_AB, RG_WIDTH), jnp.float32, 0.9, 0.999)
    a0 = u ** (1.0 / RG_C)
    rg_lambda = jnp.log(a0) - jnp.log1p(-a0)
    return {
        'x_prompt': nrm((BATCH, SEQ, D_MODEL), 1.0),
        'x_sample': nrm((DEC_BATCH, DEC_SEQ, D_MODEL), 1.0),
        'cache_sb_k': nrm((N_AB, DEC_BATCH, SB_HEADS, PAST_LEN, SB_HEAD_DIM), 1.0),
        'cache_sb_v': nrm((N_AB, DEC_BATCH, SB_HEADS, PAST_LEN, SB_HEAD_DIM), 1.0),
        'state_rg_conv': nrm((N_AB, DEC_BATCH, RG_CONV_W - 1, RG_WIDTH), 1.0),
        'state_rg_h': nrm((N_AB, DEC_BATCH, RG_WIDTH), 0.5),
        'state_cm_conv': nrm((N_CM, DEC_BATCH, CM_CONV_W - 1, D_MODEL), 0.5),
        'p_prompt': nrm((DEPTH, BATCH, SEQ, PLE_DIM), 1.0),
        'p_sample': nrm((DEPTH, DEC_BATCH, DEC_SEQ, PLE_DIM), 1.0),
        'norm_ffn': gain((DEPTH, 2, D_MODEL)),
        'w_ffn_gate': nrm((DEPTH, 2, D_MODEL, D_FF), D_MODEL ** -0.5),
        'w_ffn_up': nrm((DEPTH, 2, D_MODEL, D_FF), D_MODEL ** -0.5),
        'w_ffn_down': nrm((DEPTH, 2, D_FF, D_MODEL), D_FF ** -0.5),
        'norm_mix': gain((DEPTH, D_MODEL)),
        'w_ab_in': nrm((N_AB, D_MODEL, AB_IN_WIDTH), D_MODEL ** -0.5),
        'w_ab_out': nrm((N_AB, MIX_WIDTH, D_MODEL), MIX_WIDTH ** -0.5),
        'rg_conv_w': nrm((N_AB, RG_CONV_W, RG_WIDTH), RG_CONV_W ** -0.5),
        'rg_conv_b': nrm((N_AB, RG_WIDTH), 0.01),
        'rg_gate_a_w': nrm((N_AB, RG_BLOCKS, RG_BLOCK_DIM, RG_BLOCK_DIM), RG_BLOCK_DIM ** -0.5),
        'rg_gate_a_b': nrm((N_AB, RG_WIDTH), 0.01),
        'rg_gate_x_w': nrm((N_AB, RG_BLOCKS, RG_BLOCK_DIM, RG_BLOCK_DIM), RG_BLOCK_DIM ** -0.5),
        'rg_gate_x_b': nrm((N_AB, RG_WIDTH), 0.01),
        'rg_lambda': rg_lambda,
        'cm_pw1_w': nrm((N_CM, D_MODEL, 2 * D_MODEL), D_MODEL ** -0.5),
        'cm_pw1_b': nrm((N_CM, 2 * D_MODEL), 0.01),
        'cm_dw_w': nrm((N_CM, CM_CONV_W, D_MODEL), CM_CONV_W ** -0.5),
        'cm_dw_b': nrm((N_CM, D_MODEL), 0.01),
        'cm_ln_g': gain((N_CM, D_MODEL)),
        'cm_ln_b': nrm((N_CM, D_MODEL), 0.01),
        'cm_pw2_w': nrm((N_CM, D_MODEL, D_MODEL), D_MODEL ** -0.5),
        'cm_pw2_b': nrm((N_CM, D_MODEL), 0.01),
        'norm_ple': gain((DEPTH, D_MODEL)),
        'w_ple_gate': nrm((DEPTH, D_MODEL, D_MODEL), D_MODEL ** -0.5),
        'w_ple_proj': nrm((DEPTH, PLE_DIM, D_MODEL), PLE_DIM ** -0.5),
        'norm_final': gain((D_MODEL,)),
    }


def reference(x_prompt, x_sample, cache_sb_k, cache_sb_v, state_rg_conv, state_rg_h, state_cm_conv,
              p_prompt, p_sample, norm_ffn, w_ffn_gate, w_ffn_up, w_ffn_down, norm_mix, w_ab_in, w_ab_out,
              rg_conv_w, rg_conv_b, rg_gate_a_w, rg_gate_a_b, rg_gate_x_w, rg_gate_x_b, rg_lambda,
              cm_pw1_w, cm_pw1_b, cm_dw_w, cm_dw_b, cm_ln_g, cm_ln_b, cm_pw2_w, cm_pw2_b,
              norm_ple, w_ple_gate, w_ple_proj, norm_final):
    w = dict(norm_ffn=norm_ffn, w_ffn_gate=w_ffn_gate, w_ffn_up=w_ffn_up, w_ffn_down=w_ffn_down,
             norm_mix=norm_mix, w_ab_in=w_ab_in, w_ab_out=w_ab_out, rg_conv_w=rg_conv_w, rg_conv_b=rg_conv_b,
             rg_gate_a_w=rg_gate_a_w, rg_gate_a_b=rg_gate_a_b, rg_gate_x_w=rg_gate_x_w, rg_gate_x_b=rg_gate_x_b,
             rg_lambda=rg_lambda, cm_pw1_w=cm_pw1_w, cm_pw1_b=cm_pw1_b, cm_dw_w=cm_dw_w, cm_dw_b=cm_dw_b,
             cm_ln_g=cm_ln_g, cm_ln_b=cm_ln_b, cm_pw2_w=cm_pw2_w, cm_pw2_b=cm_pw2_b,
             norm_ple=norm_ple, w_ple_gate=w_ple_gate, w_ple_proj=w_ple_proj, norm_final=norm_final)
    y_prompt, k_p, v_p, rc_p, rh_p, cc_p = trunk(x_prompt, p_prompt, w, None, None, None, None, None)
    y_sample, k_s, v_s, rc_s, rh_s, cc_s = trunk(x_sample, p_sample, w, cache_sb_k, cache_sb_v,
                                                 state_rg_conv, state_rg_h, state_cm_conv)
    return (y_prompt, y_sample, k_p, v_p, rc_p, rh_p, cc_p, k_s, v_s, rc_s, rh_s, cc_s)
```

```python
import functools
import math

import jax
import jax.numpy as jnp
from jax import lax
from jax.experimental import pallas as pl
from jax.experimental.pallas import tpu as pltpu

NORM_EPS = 1e-6
SB_HEAD_DIM = 64
RG_C = 8.0
LANES = 128
SUBLANES = 8
MXU_DIM = 256
VMEM_LIMIT_BYTES = 56 * 1024 * 1024
EXP_UNDERFLOW = -105.0

BF16 = jnp.bfloat16
F32 = jnp.float32


def _dot(a, b):
    return jnp.dot(a, b, preferred_element_type=F32)


def _dot_nt(a, b):
    return lax.dot_general(a, b, (((1,), (1,)), ((), ())), preferred_element_type=F32)


def _sigmoid(x):
    return 1.0 / (1.0 + jnp.exp(-x))


def _softplus(x):
    return jnp.maximum(x, 0.0) + jnp.log1p(jnp.exp(-jnp.abs(x)))


def _neg_expm1(t):
    series = 1.0 / 40320.0
    for c in (1.0 / 5040.0, 1.0 / 720.0, 1.0 / 120.0, 1.0 / 24.0, 1.0 / 6.0, 0.5, 1.0):
        series = series * t + c
    return jnp.where(t > -0.3, -t * series, 1.0 - jnp.exp(t))


def _rms(x, g):
    return x * lax.rsqrt(jnp.mean(x * x, axis=-1, keepdims=True) + NORM_EPS) * g


def _gelu_tanh(x):
    return 0.5 * x * (1.0 + jnp.tanh(math.sqrt(2.0 / math.pi) * (x + 0.044715 * (x * x * x))))


def _const_spec(shape):
    zeros = (0,) * len(shape)
    return pl.BlockSpec(shape, lambda *_: zeros, pipeline_mode=pl.Buffered(1))


def _params(semantics):
    return pltpu.CompilerParams(dimension_semantics=semantics, vmem_limit_bytes=VMEM_LIMIT_BYTES)


def _row_tile(n, target):
    t = min(n, target)
    assert n % t == 0, (n, t)
    return t


def _ffn_kernel(*refs, has_mix, has_ple, has_final, ff_splits):
    it = iter(refs)
    h_ref = next(it)
    if has_mix:
        sb_ref, rg_ref, wo_ref = next(it), next(it), next(it)
    g_ref, wg_ref, wu_ref, wd_ref = next(it), next(it), next(it), next(it)
    if has_ple:
        p_ref, gp_ref, wpg_ref, wpp_ref = next(it), next(it), next(it), next(it)
    if has_final:
        gf_ref = next(it)
    o_ref = next(it)

    h = h_ref[...]
    if has_mix:
        half = sb_ref.shape[-1]
        h = h + _dot(sb_ref[...], wo_ref[:half, :]) + _dot(rg_ref[...], wo_ref[half:, :])
    xn = _rms(h, g_ref[...]).astype(BF16)
    acc = None
    for s, e in ff_splits:
        gate = _dot(xn, wg_ref[:, s:e])
        up = _dot(xn, wu_ref[:, s:e])
        act = (gate * _sigmoid(gate) * up).astype(BF16)
        part = _dot(act, wd_ref[s:e, :])
        acc = part if acc is None else acc + part
    h = h + 0.5 * acc
    if has_ple:
        gate = _sigmoid(_dot(_rms(h, gp_ref[...]).astype(BF16), wpg_ref[...]))
        h = h + gate * _dot(p_ref[...].astype(BF16), wpp_ref[...])
    if has_final:
        h = _rms(h, gf_ref[...])
    o_ref[...] = h


def _ff_splits(d_ff):
    if d_ff % MXU_DIM == 0 and d_ff // MXU_DIM >= 2:
        mid = (d_ff // MXU_DIM + 1) // 2 * MXU_DIM
        return ((0, mid), (mid, d_ff))
    return ((0, d_ff),)


def _ffn_call(h, g, wg, wu, wd, mix=None, ple=None, final_g=None, tile=512):
    n, d = h.shape
    d_ff = wg.shape[1]
    tm = _row_tile(n, tile)
    row = lambda w: pl.BlockSpec((tm, w), lambda i: (i, 0))
    args, specs = [h], [row(d)]
    if mix is not None:
        sb, rg, wo = mix
        args += [sb, rg, wo]
        specs += [row(sb.shape[1]), row(rg.shape[1]), _const_spec(wo.shape)]
    args += [g, wg, wu, wd]
    specs += [_const_spec(g.shape), _const_spec(wg.shape), _const_spec(wu.shape), _const_spec(wd.shape)]
    if ple is not None:
        p, gp, wpg, wpp = ple
        args += [p, gp, wpg, wpp]
        specs += [row(p.shape[1]), _const_spec(gp.shape), _const_spec(wpg.shape), _const_spec(wpp.shape)]
    if final_g is not None:
        args.append(final_g)
        specs.append(_const_spec(final_g.shape))
    kern = functools.partial(_ffn_kernel, has_mix=mix is not None, has_ple=ple is not None,
                             has_final=final_g is not None, ff_splits=_ff_splits(d_ff))
    return pl.pallas_call(
        kern, grid=(n // tm,), in_specs=specs, out_specs=row(d),
        out_shape=jax.ShapeDtypeStruct((n, d), F32),
        compiler_params=_params(("parallel",)), name="ffn",
    )(*args)


def _inproj_kernel(h_ref, g_ref, w_ref, qkv_ref, k_ref, v_ref, xrg_ref, *, sb_width, seq_rows):
    hn = _rms(h_ref[...], g_ref[...]).astype(BF16)
    u = _dot(hn, w_ref[...])
    scale = SB_HEAD_DIM ** -0.5
    qkv_ref[:, :sb_width] = (u[:, :sb_width] * scale).astype(BF16)
    qkv_ref[:, sb_width:] = u[:, sb_width:3 * sb_width].astype(BF16)
    xrg_ref[...] = u[:, 3 * sb_width:]
    n_seq, n_heads = k_ref.shape[0], k_ref.shape[1]
    for s in range(n_seq):
        rows = slice(s * seq_rows, (s + 1) * seq_rows)
        for hd in range(n_heads):
            c0 = sb_width + hd * SB_HEAD_DIM
            k_ref[s, hd] = u[rows, c0:c0 + SB_HEAD_DIM]
            v_ref[s, hd] = u[rows, c0 + sb_width:c0 + sb_width + SB_HEAD_DIM]


def _inproj_call(h, g, w_in, bsz, t_len, sb_width, tile=512):
    n, d = h.shape
    width = w_in.shape[1]
    n_heads = sb_width // SB_HEAD_DIM
    tm = _row_tile(n, tile)
    if tm <= t_len:
        assert t_len % tm == 0
        per = t_len // tm
        n_seq, seq_rows = 1, tm
        kv_map = lambda i: (i // per, 0, i % per, 0)
    else:
        assert tm % t_len == 0
        n_seq, seq_rows = tm // t_len, t_len
        kv_map = lambda i: (i, 0, 0, 0)
    kv_spec = pl.BlockSpec((n_seq, n_heads, seq_rows, SB_HEAD_DIM), kv_map)
    kv_shape = jax.ShapeDtypeStruct((bsz, n_heads, t_len, SB_HEAD_DIM), F32)
    row = lambda w: pl.BlockSpec((tm, w), lambda i: (i, 0))
    kern = functools.partial(_inproj_kernel, sb_width=sb_width, seq_rows=seq_rows)
    return pl.pallas_call(
        kern, grid=(n // tm,),
        in_specs=[row(d), _const_spec(g.shape), _const_spec(w_in.shape)],
        out_specs=[row(3 * sb_width), kv_spec, kv_spec, row(width - 3 * sb_width)],
        out_shape=[jax.ShapeDtypeStruct((n, 3 * sb_width), BF16), kv_shape, kv_shape,
                   jax.ShapeDtypeStruct((n, width - 3 * sb_width), F32)],
        compiler_params=_params(("parallel",)), name="inproj",
    )(h, g, w_in)


def _attn_kernel(q_ref, k_ref, v_ref, o_ref, *, tq, tk, offset):
    i = pl.program_id(2)
    q2 = q_ref[0]
    lane_q = lax.broadcasted_iota(jnp.int32, (tq, LANES), 1)
    lane_k = lax.broadcasted_iota(jnp.int32, (tk, LANES), 1)
    zero = jnp.zeros((), BF16)
    q_heads = (jnp.where(lane_q < SB_HEAD_DIM, q2, zero), jnp.where(lane_q >= SB_HEAD_DIM, q2, zero))

    r_i = lax.broadcasted_iota(jnp.int32, (tk, tk), 0)
    c_i = lax.broadcasted_iota(jnp.int32, (tk, tk), 1)
    tri = jnp.where(r_i > c_i, 1.0, 0.0).astype(BF16)
    sum_mat = jnp.concatenate([tri, jnp.ones((tk, tk), BF16)], axis=1)
    sum_mat3 = jnp.concatenate([sum_mat, sum_mat, sum_mat], axis=0)

    q0 = offset + i * tq
    j_diag = q0 // tk
    q_pos = q0 + lax.broadcasted_iota(jnp.int32, (tq, tk), 0)
    k_off = lax.broadcasted_iota(jnp.int32, (tq, tk), 1)

    def block(j, carries, acc, masked):
        start = pl.multiple_of(j * tk, tk)
        kb = k_ref[0, pl.ds(start, tk), :]
        vb = v_ref[0, pl.ds(start, tk), :]
        if masked:
            mask = (k_off + j * tk) < q_pos
        probs, new_carries = [], []
        for qh, carry in zip(q_heads, carries):
            z = _dot_nt(qh, kb)
            log_keep = -_softplus(z)
            log_beta = z + log_keep
            if masked:
                log_keep = jnp.where(mask, log_keep, 0.0)
            hi = log_keep.astype(BF16)
            rem = log_keep - hi.astype(F32)
            mid = rem.astype(BF16)
            lo = (rem - mid.astype(F32)).astype(BF16)
            sums = _dot(jnp.concatenate([hi, mid, lo], axis=1), sum_mat3)
            a = jnp.exp(log_beta + sums[:, :tk] + carry)
            if masked:
                a = jnp.where(mask, a, 0.0)
            probs.append(a.astype(BF16))
            new_carries.append(carry + sums[:, tk:])
        v_heads = jnp.concatenate([jnp.where(lane_k < SB_HEAD_DIM, vb, zero),
                                   jnp.where(lane_k >= SB_HEAD_DIM, vb, zero)], axis=0)
        acc = acc + _dot(jnp.concatenate(probs, axis=1), v_heads)
        return tuple(new_carries), acc

    def live(carries):
        return (jnp.max(jnp.maximum(carries[0], carries[1])) >= EXP_UNDERFLOW).astype(jnp.int32)

    zeros = jnp.zeros((tq, tk), F32)
    carries, acc = block(j_diag, (zeros, zeros), jnp.zeros((tq, LANES), F32), True)

    def cond(state):
        j, go = state[0], state[1]
        return jnp.logical_and(j >= 0, go > 0)

    def body(state):
        j, _, c0, c1, acc = state
        (c0, c1), acc = block(j, (c0, c1), acc, False)
        return j - 1, live((c0, c1)), c0, c1, acc

    state = lax.while_loop(cond, body, (j_diag - 1, live(carries), carries[0], carries[1], acc))
    o_ref[0] = state[4].astype(o_ref.dtype)


def _attn_call(q_arr, q_blk0, k_arr, k_blk0, v_arr, v_blk0, sb_width, offset, tq, tk):
    bsz, t_q, _ = q_arr.shape
    t_k = k_arr.shape[1]
    assert t_q % tq == 0 and t_k % tk == 0 and tk % tq == 0 and offset % tq == 0
    assert offset + t_q <= t_k
    n_pairs = sb_width // LANES
    kern = functools.partial(_attn_kernel, tq=tq, tk=tk, offset=offset)
    kv_spec = lambda blk0: pl.BlockSpec((1, t_k, LANES), lambda b, hp, i: (b, 0, blk0 + hp))
    return pl.pallas_call(
        kern, grid=(bsz, n_pairs, t_q // tq),
        in_specs=[pl.BlockSpec((1, tq, LANES), lambda b, hp, i: (b, i, q_blk0 + hp)),
                  kv_spec(k_blk0), kv_spec(v_blk0)],
        out_specs=pl.BlockSpec((1, tq, LANES), lambda b, hp, i: (b, i, hp)),
        out_shape=jax.ShapeDtypeStruct((bsz, t_q, sb_width), BF16),
        compiler_params=_params(("parallel", "parallel", "arbitrary")), name="stickbreak",
    )(q_arr, k_arr, v_arr)


def _rg_kernel(xr_ref, xg_ref, cst_ref, h0_ref, cw_ref, cb_ref, wa_ref, ba_ref, wx_ref, bx_ref,
               lam_ref, rg_ref, cnew_ref, hlast_ref, hist_scr, hc_scr, *, tt):
    ti = pl.program_id(1)

    @pl.when(ti == 0)
    def _():
        hist_scr[...] = cst_ref[0]
        hc_scr[...] = h0_ref[0]

    x = xr_ref[0]
    conv_w = cw_ref.shape[0]
    xcat = jnp.concatenate([hist_scr[...], x], axis=0)
    xc = cb_ref[...] + cw_ref[conv_w - 1:conv_w, :] * x
    for k in range(1, conv_w):
        xc = xc + cw_ref[conv_w - 1 - k:conv_w - k, :] * pltpu.roll(xcat, k, axis=0)[SUBLANES:]
    hist_scr[...] = xcat[tt:]

    xcb = xc.astype(BF16)
    r = _sigmoid(_dot(xcb, wa_ref[...]) + ba_ref[...])
    ig = _sigmoid(_dot(xcb, wx_ref[...]) + bx_ref[...])
    log_a = (-RG_C * _softplus(-lam_ref[...])) * r
    a = jnp.exp(log_a)
    b = jnp.sqrt(_neg_expm1(2.0 * log_a)) * (ig * xc)

    row = lax.broadcasted_iota(jnp.int32, a.shape, 0)
    s = 1
    while s < tt:
        valid = row >= s
        b = b + a * jnp.where(valid, pltpu.roll(b, s, axis=0), 0.0)
        a = a * jnp.where(valid, pltpu.roll(a, s, axis=0), 1.0)
        s *= 2
    h = b + a * hc_scr[SUBLANES - 1:SUBLANES, :]
    hc_scr[...] = h[tt - SUBLANES:]
    rg_ref[0] = (h * _gelu_tanh(xg_ref[0])).astype(rg_ref.dtype)

    @pl.when(ti == pl.num_programs(1) - 1)
    def _():
        cnew_ref[0] = xcat[tt:]
        hlast_ref[0] = h[tt - SUBLANES:]


def _rg_call(xrg, conv_state8, h0_8, cw, cb, wa, ba, wx, bx, lam, tile=256):
    bsz, t_len, two_w = xrg.shape
    w = two_w // 2
    tt = _row_tile(t_len, tile)
    kern = functools.partial(_rg_kernel, tt=tt)
    state_spec = pl.BlockSpec((1, SUBLANES, w), lambda b, t: (b, 0, 0))
    state_shape = jax.ShapeDtypeStruct((bsz, SUBLANES, w), F32)
    return pl.pallas_call(
        kern, grid=(bsz, t_len // tt),
        in_specs=[pl.BlockSpec((1, tt, w), lambda b, t: (b, t, 0)),
                  pl.BlockSpec((1, tt, w), lambda b, t: (b, t, 1)),
                  state_spec, state_spec,
                  _const_spec(cw.shape), _const_spec(cb.shape), _const_spec(wa.shape), _const_spec(ba.shape),
                  _const_spec(wx.shape), _const_spec(bx.shape), _const_spec(lam.shape)],
        out_specs=[pl.BlockSpec((1, tt, w), lambda b, t: (b, t, 0)), state_spec, state_spec],
        out_shape=[jax.ShapeDtypeStruct((bsz, t_len, w), BF16), state_shape, state_shape],
        scratch_shapes=[pltpu.VMEM((SUBLANES, w), F32), pltpu.VMEM((SUBLANES, w), F32)],
        compiler_params=_params(("parallel", "arbitrary")), name="rglru",
    )(xrg, xrg, conv_state8, h0_8, cw, cb, wa, ba, wx, bx, lam)


def _convmod_kernel(h_ref, cst_ref, g_ref, w1_ref, b1_ref, dw_ref, dwb_ref, lng_ref, lnb_ref,
                    w2_ref, b2_ref, o_ref, cnew_ref, ucat_scr, conv_scr, *, tt, hist, taps):
    ti = pl.program_id(1)

    @pl.when(ti == 0)
    def _():
        ucat_scr[:hist, :] = cst_ref[0]

    x = h_ref[0]
    d = x.shape[1]
    hn = _rms(x, g_ref[...]).astype(BF16)
    ag = _dot(hn, w1_ref[...]) + b1_ref[...]
    ucat_scr[hist:, :] = ag[:, :d] * _sigmoid(ag[:, d:])

    for c in range(d // LANES):
        cols = slice(c * LANES, (c + 1) * LANES)
        ucat = ucat_scr[:, cols]
        acc = jnp.broadcast_to(dwb_ref[:, cols], (tt, LANES))
        for r in range(SUBLANES):
            base = ucat if r == 0 else pltpu.roll(ucat, r, axis=0)
            for m in range((taps - 1 - r) // SUBLANES + 1):
                k = SUBLANES * m + r
                lo = hist - SUBLANES * m
                acc = acc + dw_ref[taps - 1 - k:taps - k, cols] * base[lo:lo + tt]
        conv_scr[:, cols] = acc
    new_hist = ucat_scr[tt:, :]
    ucat_scr[:hist, :] = new_hist

    cv = conv_scr[...]
    mu = jnp.mean(cv, axis=-1, keepdims=True)
    xc = cv - mu
    ln = xc * lax.rsqrt(jnp.mean(xc * xc, axis=-1, keepdims=True) + NORM_EPS) * lng_ref[...] + lnb_ref[...]
    act = (ln * _sigmoid(ln)).astype(BF16)
    o_ref[0] = x + _dot(act, w2_ref[...]) + b2_ref[...]

    @pl.when(ti == pl.num_programs(1) - 1)
    def _():
        cnew_ref[0] = new_hist


def _convmod_call(h, conv_state_pad, g, w1, b1, dw, dwb, lng, lnb, w2, b2, tile=256):
    bsz, t_len, d = h.shape
    hist = conv_state_pad.shape[1]
    taps = dw.shape[0]
    tt = _row_tile(t_len, tile)
    assert tt >= hist and hist % SUBLANES == 0 and taps - 1 <= hist
    kern = functools.partial(_convmod_kernel, tt=tt, hist=hist, taps=taps)
    state_spec = pl.BlockSpec((1, hist, d), lambda b, t: (b, 0, 0))
    consts = [g, w1, b1, dw, dwb, lng, lnb, w2, b2]
    return pl.pallas_call(
        kern, grid=(bsz, t_len // tt),
        in_specs=[pl.BlockSpec((1, tt, d), lambda b, t: (b, t, 0)), state_spec]
                 + [_const_spec(c.shape) for c in consts],
        out_specs=[pl.BlockSpec((1, tt, d), lambda b, t: (b, t, 0)), state_spec],
        out_shape=[jax.ShapeDtypeStruct((bsz, t_len, d), F32),
                   jax.ShapeDtypeStruct((bsz, hist, d), F32)],
        scratch_shapes=[pltpu.VMEM((hist + tt, d), F32), pltpu.VMEM((tt, d), F32)],
        compiler_params=_params(("parallel", "arbitrary")), name="convmod",
    )(h, conv_state_pad, *consts)


def _block_diag(w):
    g, i, j = w.shape
    eye = jnp.eye(g, dtype=w.dtype)
    return (eye[:, None, :, None] * w[:, :, None, :]).reshape(g * i, g * j)


def _pad_rows_front(x, rows):
    return jnp.pad(x, ((0, 0), (rows - x.shape[1], 0), (0, 0)))


def _trunk(x, p, w, sb_k, sb_v, rg_conv, rg_h, cm_conv):
    bsz, t_len, d = x.shape
    n = bsz * t_len
    fresh = sb_k is None
    depth = w['norm_ffn'].shape[0]
    row2 = lambda v: v.reshape(1, -1)
    bf = lambda v: v.astype(BF16)
    h = x.reshape(n, d)
    out_k, out_v, out_rc, out_rh, out_cc = [], [], [], [], []
    for i in range(depth):
        j = i // 2
        ffn = lambda hh, s, **kw: _ffn_call(
            hh, row2(w['norm_ffn'][i, s]), bf(w['w_ffn_gate'][i, s]), bf(w['w_ffn_up'][i, s]),
            bf(w['w_ffn_down'][i, s]), **kw)
        ple = (p[i].reshape(n, -1), row2(w['norm_ple'][i]), bf(w['w_ple_gate'][i]), bf(w['w_ple_proj'][i]))
        final_g = row2(w['norm_final']) if i == depth - 1 else None
        h = ffn(h, 0)
        if i % 2 == 0:
            rg_w = w['rg_lambda'].shape[1]
            sb_w = (w['w_ab_in'].shape[2] - 2 * rg_w) // 3
            qkv, k_new, v_new, xrg = _inproj_call(h, row2(w['norm_mix'][i]), bf(w['w_ab_in'][j]),
                                                  bsz, t_len, sb_w)
            qkv3 = qkv.reshape(bsz, t_len, 3 * sb_w)
            n_pairs = sb_w // LANES
            if fresh:
                sb = _attn_call(qkv3, 0, qkv3, n_pairs, qkv3, 2 * n_pairs, sb_w, 0, 128, 128)
                conv0 = jnp.zeros((bsz, SUBLANES, rg_w), F32)
                h0 = jnp.zeros((bsz, SUBLANES, rg_w), F32)
            else:
                past = sb_k.shape[3]
                tk = 128
                t_k = -(-(past + t_len) // tk) * tk

                def with_cache(cache, blk0):
                    tm_major = cache.transpose(0, 2, 1, 3).reshape(bsz, past, sb_w).astype(BF16)
                    new = qkv3[:, :, blk0 * LANES:blk0 * LANES + sb_w]
                    full = jnp.concatenate([tm_major, new], axis=1)
                    return jnp.pad(full, ((0, 0), (0, t_k - past - t_len), (0, 0)))

                sb = _attn_call(qkv3, 0, with_cache(sb_k[j], n_pairs), 0, with_cache(sb_v[j], 2 * n_pairs), 0,
                                sb_w, past, t_len, tk)
                conv0 = _pad_rows_front(rg_conv[j], SUBLANES)
                h0 = jnp.broadcast_to(rg_h[j][:, None, :], (bsz, SUBLANES, rg_w))
            rg, rc8, rh8 = _rg_call(
                xrg.reshape(bsz, t_len, 2 * rg_w), conv0, h0, w['rg_conv_w'][j], row2(w['rg_conv_b'][j]),
                bf(_block_diag(w['rg_gate_a_w'][j])), row2(w['rg_gate_a_b'][j]),
                bf(_block_diag(w['rg_gate_x_w'][j])), row2(w['rg_gate_x_b'][j]), row2(w['rg_lambda'][j]))
            out_k.append(k_new)
            out_v.append(v_new)
            out_rc.append(rc8[:, SUBLANES - (w['rg_conv_w'].shape[1] - 1):])
            out_rh.append(rh8[:, SUBLANES - 1])
            mix = (sb.reshape(n, sb_w), rg.reshape(n, rg_w), bf(w['w_ab_out'][j]))
            h = ffn(h, 1, mix=mix, ple=ple, final_g=final_g)
        else:
            taps = w['cm_dw_w'].shape[1]
            hist = -(-(taps - 1) // SUBLANES) * SUBLANES
            cc0 = jnp.zeros((bsz, hist, d), F32) if fresh else _pad_rows_front(cm_conv[j], hist)
            h3, cc_new = _convmod_call(
                h.reshape(bsz, t_len, d), cc0, row2(w['norm_mix'][i]), bf(w['cm_pw1_w'][j]),
                row2(w['cm_pw1_b'][j]), w['cm_dw_w'][j], row2(w['cm_dw_b'][j]), row2(w['cm_ln_g'][j]),
                row2(w['cm_ln_b'][j]), bf(w['cm_pw2_w'][j]), row2(w['cm_pw2_b'][j]))
            out_cc.append(cc_new[:, hist - (taps - 1):])
            h = ffn(h3.reshape(n, d), 1, ple=ple, final_g=final_g)
    y = h.reshape(bsz, t_len, d)
    return y, jnp.stack(out_k), jnp.stack(out_v), jnp.stack(out_rc), jnp.stack(out_rh), jnp.stack(out_cc)


def kernel(x_prompt, x_sample, cache_sb_k, cache_sb_v, state_rg_conv, state_rg_h, state_cm_conv,
           p_prompt, p_sample, norm_ffn, w_ffn_gate, w_ffn_up, w_ffn_down, norm_mix, w_ab_in, w_ab_out,
           rg_conv_w, rg_conv_b, rg_gate_a_w, rg_gate_a_b, rg_gate_x_w, rg_gate_x_b, rg_lambda,
           cm_pw1_w, cm_pw1_b, cm_dw_w, cm_dw_b, cm_ln_g, cm_ln_b, cm_pw2_w, cm_pw2_b,
           norm_ple, w_ple_gate, w_ple_proj, norm_final):
    w = dict(norm_ffn=norm_ffn, w_ffn_gate=w_ffn_gate, w_ffn_up=w_ffn_up, w_ffn_down=w_ffn_down,
             norm_mix=norm_mix, w_ab_in=w_ab_in, w_ab_out=w_ab_out, rg_conv_w=rg_conv_w, rg_conv_b=rg_conv_b,
             rg_gate_a_w=rg_gate_a_w, rg_gate_a_b=rg_gate_a_b, rg_gate_x_w=rg_gate_x_w, rg_gate_x_b=rg_gate_x_b,
             rg_lambda=rg_lambda, cm_pw1_w=cm_pw1_w, cm_pw1_b=cm_pw1_b, cm_dw_w=cm_dw_w, cm_dw_b=cm_dw_b,
             cm_ln_g=cm_ln_g, cm_ln_b=cm_ln_b, cm_pw2_w=cm_pw2_w, cm_pw2_b=cm_pw2_b,
             norm_ple=norm_ple, w_ple_gate=w_ple_gate, w_ple_proj=w_ple_proj, norm_final=norm_final)
    y_p, k_p, v_p, rc_p, rh_p, cc_p = _trunk(x_prompt, p_prompt, w, None, None, None, None, None)
    y_s, k_s, v_s, rc_s, rh_s, cc_s = _trunk(x_sample, p_sample, w, cache_sb_k, cache_sb_v,
                                             state_rg_conv, state_rg_h, state_cm_conv)
    return (y_p, y_s, k_p, v_p, rc_p, rh_p, cc_p, k_s, v_s, rc_s, rh_s, cc_s)
```

```python
import functools
import math

import jax
import jax.numpy as jnp
from jax import lax
from jax.experimental import pallas as pl
from jax.experimental.pallas import tpu as pltpu

NORM_EPS = 1e-6
SB_HEAD_DIM = 64
RG_C = 8.0
LANES = 128
SUBLANES = 8
MXU_DIM = 256
VMEM_LIMIT_BYTES = 56 * 1024 * 1024
EXP_UNDERFLOW = -105.0
CONV_ROW_CHUNK = 64

BF16 = jnp.bfloat16
F32 = jnp.float32


def _dot(a, b):
    return jnp.dot(a, b, preferred_element_type=F32)


def _dot_nt(a, b):
    return lax.dot_general(a, b, (((1,), (1,)), ((), ())), preferred_element_type=F32)


def _sigmoid(x):
    return 1.0 / (1.0 + jnp.exp(-x))


def _softplus(x):
    return jnp.maximum(x, 0.0) + jnp.log(1.0 + jnp.exp(-jnp.abs(x)))


def _log_sigmoids(z):
    nz = -z
    log_keep = jnp.minimum(nz, 0.0) - jnp.log(1.0 + jnp.exp(jnp.minimum(z, nz)))
    return log_keep, z + log_keep


def _neg_expm1(t):
    series = 1.0 / 40320.0
    for c in (1.0 / 5040.0, 1.0 / 720.0, 1.0 / 120.0, 1.0 / 24.0, 1.0 / 6.0, 0.5, 1.0):
        series = series * t + c
    return jnp.where(t > -0.3, -t * series, 1.0 - jnp.exp(t))


def _rms(x, g):
    return x * lax.rsqrt(jnp.mean(x * x, axis=-1, keepdims=True) + NORM_EPS) * g


def _gelu_tanh(x):
    return 0.5 * x * (1.0 + jnp.tanh(math.sqrt(2.0 / math.pi) * (x + 0.044715 * (x * x * x))))


def _const_spec(shape):
    zeros = (0,) * len(shape)
    return pl.BlockSpec(shape, lambda *_: zeros, pipeline_mode=pl.Buffered(1))


def _params(semantics):
    return pltpu.CompilerParams(dimension_semantics=semantics, vmem_limit_bytes=VMEM_LIMIT_BYTES)


def _row_tile(n, target):
    t = min(n, target)
    assert n % t == 0, (n, t)
    return t


def _ffn_kernel(*refs, has_mix, has_ple, has_final, ff_splits):
    it = iter(refs)
    h_ref = next(it)
    if has_mix:
        sb_ref, rg_ref, wo_ref = next(it), next(it), next(it)
    g_ref, wg_ref, wu_ref, wd_ref = next(it), next(it), next(it), next(it)
    if has_ple:
        p_ref, gp_ref, wpg_ref, wpp_ref = next(it), next(it), next(it), next(it)
    if has_final:
        gf_ref = next(it)
    o_ref = next(it)

    h = h_ref[...]
    if has_mix:
        half = sb_ref.shape[-1]
        h = h + _dot(sb_ref[...], wo_ref[:half, :]) + _dot(rg_ref[...], wo_ref[half:, :])
    xn = _rms(h, g_ref[...]).astype(BF16)
    acc = None
    for s, e in ff_splits:
        gate = _dot(xn, wg_ref[:, s:e])
        up = _dot(xn, wu_ref[:, s:e])
        act = (gate * _sigmoid(gate) * up).astype(BF16)
        part = _dot(act, wd_ref[s:e, :])
        acc = part if acc is None else acc + part
    h = h + 0.5 * acc
    if has_ple:
        gate = _sigmoid(_dot(_rms(h, gp_ref[...]).astype(BF16), wpg_ref[...]))
        h = h + gate * _dot(p_ref[...].astype(BF16), wpp_ref[...])
    if has_final:
        h = _rms(h, gf_ref[...])
    o_ref[...] = h


def _ff_splits(d_ff):
    if d_ff % MXU_DIM == 0 and d_ff // MXU_DIM >= 2:
        mid = (d_ff // MXU_DIM + 1) // 2 * MXU_DIM
        return ((0, mid), (mid, d_ff))
    return ((0, d_ff),)


def _ffn_call(h, g, wg, wu, wd, mix=None, ple=None, final_g=None, tile=512):
    n, d = h.shape
    d_ff = wg.shape[1]
    tm = _row_tile(n, tile)
    row = lambda w: pl.BlockSpec((tm, w), lambda i: (i, 0))
    args, specs = [h], [row(d)]
    if mix is not None:
        sb, rg, wo = mix
        args += [sb, rg, wo]
        specs += [row(sb.shape[1]), row(rg.shape[1]), _const_spec(wo.shape)]
    args += [g, wg, wu, wd]
    specs += [_const_spec(g.shape), _const_spec(wg.shape), _const_spec(wu.shape), _const_spec(wd.shape)]
    if ple is not None:
        p, gp, wpg, wpp = ple
        args += [p, gp, wpg, wpp]
        specs += [row(p.shape[1]), _const_spec(gp.shape), _const_spec(wpg.shape), _const_spec(wpp.shape)]
    if final_g is not None:
        args.append(final_g)
        specs.append(_const_spec(final_g.shape))
    kern = functools.partial(_ffn_kernel, has_mix=mix is not None, has_ple=ple is not None,
                             has_final=final_g is not None, ff_splits=_ff_splits(d_ff))
    return pl.pallas_call(
        kern, grid=(n // tm,), in_specs=specs, out_specs=row(d),
        out_shape=jax.ShapeDtypeStruct((n, d), F32),
        compiler_params=_params(("parallel",)), name="ffn",
    )(*args)


def _inproj_kernel(h_ref, g_ref, w_ref, qkv_ref, k_ref, v_ref, xrg_ref, *, sb_width, seq_rows):
    hn = _rms(h_ref[...], g_ref[...]).astype(BF16)
    u = _dot(hn, w_ref[...])
    scale = SB_HEAD_DIM ** -0.5
    qkv_ref[:, :sb_width] = (u[:, :sb_width] * scale).astype(BF16)
    qkv_ref[:, sb_width:] = u[:, sb_width:3 * sb_width].astype(BF16)
    xrg_ref[...] = u[:, 3 * sb_width:]
    n_seq, n_heads = k_ref.shape[0], k_ref.shape[1]
    for s in range(n_seq):
        rows = slice(s * seq_rows, (s + 1) * seq_rows)
        for hd in range(n_heads):
            c0 = sb_width + hd * SB_HEAD_DIM
            k_ref[s, hd] = u[rows, c0:c0 + SB_HEAD_DIM]
            v_ref[s, hd] = u[rows, c0 + sb_width:c0 + sb_width + SB_HEAD_DIM]


def _inproj_call(h, g, w_in, bsz, t_len, sb_width, tile=512):
    n, d = h.shape
    width = w_in.shape[1]
    n_heads = sb_width // SB_HEAD_DIM
    tm = _row_tile(n, tile)
    if tm <= t_len:
        assert t_len % tm == 0
        per = t_len // tm
        n_seq, seq_rows = 1, tm
        kv_map = lambda i: (i // per, 0, i % per, 0)
    else:
        assert tm % t_len == 0
        n_seq, seq_rows = tm // t_len, t_len
        kv_map = lambda i: (i, 0, 0, 0)
    kv_spec = pl.BlockSpec((n_seq, n_heads, seq_rows, SB_HEAD_DIM), kv_map)
    kv_shape = jax.ShapeDtypeStruct((bsz, n_heads, t_len, SB_HEAD_DIM), F32)
    row = lambda w: pl.BlockSpec((tm, w), lambda i: (i, 0))
    kern = functools.partial(_inproj_kernel, sb_width=sb_width, seq_rows=seq_rows)
    return pl.pallas_call(
        kern, grid=(n // tm,),
        in_specs=[row(d), _const_spec(g.shape), _const_spec(w_in.shape)],
        out_specs=[row(3 * sb_width), kv_spec, kv_spec, row(width - 3 * sb_width)],
        out_shape=[jax.ShapeDtypeStruct((n, 3 * sb_width), BF16), kv_shape, kv_shape,
                   jax.ShapeDtypeStruct((n, width - 3 * sb_width), F32)],
        compiler_params=_params(("parallel",)), name="inproj",
    )(h, g, w_in)


ATTN_KEY_BLOCK = LANES
ATTN_WINDOW = 3 * ATTN_KEY_BLOCK


def _attn_kernel(q_ref, k_ref, v_ref, o_ref, *, tq, offset):
    tk = ATTN_KEY_BLOCK
    n_sub = ATTN_WINDOW // tk
    n_pairs = q_ref.shape[2] // LANES
    i = pl.program_id(1)
    q0 = offset + i * tq
    w_end = (offset + tq + tk - 1) // tk * tk + i * tq
    w_start = pl.multiple_of(jnp.maximum(w_end - ATTN_WINDOW, 0), tk)

    zero = jnp.zeros((), BF16)
    first_q = lax.broadcasted_iota(jnp.int32, (tq, LANES), 1) < SB_HEAD_DIM
    r_i = lax.broadcasted_iota(jnp.int32, (tk, tk), 0)
    c_i = lax.broadcasted_iota(jnp.int32, (tk, tk), 1)
    tri = jnp.where(r_i > c_i, 1.0, 0.0).astype(BF16)
    sum_mat = jnp.concatenate([tri, jnp.ones((tk, tk), BF16)], axis=1)
    sum_mat = jnp.concatenate([sum_mat, sum_mat], axis=0)

    def block_sums(log_keep):
        hi = log_keep.astype(BF16)
        lo = (log_keep - hi.astype(F32)).astype(BF16)
        sums = _dot(jnp.concatenate([hi, lo], axis=1), sum_mat)
        return sums[:, :tk], sums[:, tk:]

    def q_heads(pair):
        q2 = q_ref[0, :, pair * LANES:(pair + 1) * LANES]
        return jnp.where(first_q, q2, zero), jnp.where(first_q, zero, q2)

    def v_heads(vb):
        first_k = lax.broadcasted_iota(jnp.int32, vb.shape, 1) < SB_HEAD_DIM
        return jnp.concatenate([jnp.where(first_k, vb, zero), jnp.where(first_k, zero, vb)], axis=0)

    col_minus_row = (lax.broadcasted_iota(jnp.int32, (tq, ATTN_WINDOW), 1)
                     - lax.broadcasted_iota(jnp.int32, (tq, ATTN_WINDOW), 0))
    mask = col_minus_row < (q0 - w_start)
    n_heads = 2 * n_pairs
    scores = []
    for pair in range(n_pairs):
        kw = k_ref[0, pl.ds(w_start, ATTN_WINDOW), pair * LANES:(pair + 1) * LANES]
        scores += [_dot_nt(qh, kw) for qh in q_heads(pair)]
    log_betas, splits = [], []
    for z in scores:
        log_keep, log_beta = _log_sigmoids(z)
        log_keep = jnp.where(mask, log_keep, 0.0)
        hi = log_keep.astype(BF16)
        lo = (log_keep - hi.astype(F32)).astype(BF16)
        log_betas.append(log_beta)
        splits.append((hi, lo))
    sums = [[_dot(jnp.concatenate([hi[:, sub * tk:(sub + 1) * tk], lo[:, sub * tk:(sub + 1) * tk]], axis=1),
                  sum_mat) for sub in range(n_sub)] for hi, lo in splits]
    carries, probs = [], []
    for hd in range(n_heads):
        later, carry = [None] * n_sub, None
        for sub in reversed(range(n_sub)):
            local, total = sums[hd][sub][:, :tk], sums[hd][sub][:, tk:]
            later[sub] = local if carry is None else local + carry
            carry = total if carry is None else carry + total
        a = jnp.where(mask, jnp.exp(log_betas[hd] + jnp.concatenate(later, axis=1)), 0.0)
        probs.append(a.astype(BF16))
        carries.append(carry)
    accs = []
    for pair in range(n_pairs):
        vw = v_ref[0, pl.ds(w_start, ATTN_WINDOW), pair * LANES:(pair + 1) * LANES]
        accs.append(_dot(jnp.concatenate(probs[2 * pair:2 * pair + 2], axis=1), v_heads(vw)))

    def live(cs):
        m = cs[0]
        for c in cs[1:]:
            m = jnp.maximum(m, c)
        return (jnp.max(m) >= EXP_UNDERFLOW).astype(jnp.int32)

    def cond(state):
        return jnp.logical_and(state[0] >= 0, state[1] > 0)

    def body(state):
        j = state[0]
        cs, acs = list(state[2]), list(state[3])
        start = pl.multiple_of(j * tk, tk)
        for pair in range(n_pairs):
            cols = slice(pair * LANES, (pair + 1) * LANES)
            kb = k_ref[0, pl.ds(start, tk), cols]
            vb = v_ref[0, pl.ds(start, tk), cols]
            probs = []
            for hd, qh in enumerate(q_heads(pair)):
                z = _dot_nt(qh, kb)
                log_keep, log_beta = _log_sigmoids(z)
                local, total = block_sums(log_keep)
                carry = cs[2 * pair + hd]
                probs.append(jnp.exp(log_beta + local + carry).astype(BF16))
                cs[2 * pair + hd] = carry + total
            acs[pair] = acs[pair] + _dot(jnp.concatenate(probs, axis=1), v_heads(vb))
        return j - 1, live(cs), tuple(cs), tuple(acs)

    state = lax.while_loop(cond, body, (w_start // tk - 1, live(carries), tuple(carries), tuple(accs)))
    for pair in range(n_pairs):
        o_ref[0, :, pair * LANES:(pair + 1) * LANES] = state[3][pair].astype(o_ref.dtype)


def _attn_call(q_arr, q_blk, k_arr, k_blk, v_arr, v_blk, sb_width, offset, tq):
    bsz, t_q, _ = q_arr.shape
    t_k = k_arr.shape[1]
    assert t_q % tq == 0 and offset % tq == 0 and (tq == ATTN_KEY_BLOCK or tq == t_q)
    assert t_k % ATTN_KEY_BLOCK == 0 and t_k >= ATTN_WINDOW and offset + t_q <= t_k
    kern = functools.partial(_attn_kernel, tq=tq, offset=offset)
    kv_spec = lambda blk: pl.BlockSpec((1, t_k, sb_width), lambda b, i: (b, 0, blk),
                                       pipeline_mode=pl.Buffered(1))
    return pl.pallas_call(
        kern, grid=(bsz, t_q // tq),
        in_specs=[pl.BlockSpec((1, tq, sb_width), lambda b, i: (b, i, q_blk)), kv_spec(k_blk), kv_spec(v_blk)],
        out_specs=pl.BlockSpec((1, tq, sb_width), lambda b, i: (b, i, 0)),
        out_shape=jax.ShapeDtypeStruct((bsz, t_q, sb_width), BF16),
        compiler_params=_params(("parallel", "arbitrary")), name="stickbreak",
    )(q_arr, k_arr, v_arr)


def _rg_kernel(xr_ref, xg_ref, cst_ref, h0_ref, cw_ref, cb_ref, wa_ref, ba_ref, wx_ref, bx_ref,
               lam_ref, rg_ref, cnew_ref, hlast_ref, hist_scr, hc_scr, *, tt):
    ti = pl.program_id(1)

    @pl.when(ti == 0)
    def _():
        hist_scr[...] = cst_ref[0]
        hc_scr[...] = h0_ref[0]

    x = xr_ref[0]
    conv_w = cw_ref.shape[0]
    xcat = jnp.concatenate([hist_scr[...], x], axis=0)
    xc = cb_ref[...] + cw_ref[conv_w - 1:conv_w, :] * x
    for k in range(1, conv_w):
        xc = xc + cw_ref[conv_w - 1 - k:conv_w - k, :] * pltpu.roll(xcat, k, axis=0)[SUBLANES:]
    hist_scr[...] = xcat[tt:]

    xcb = xc.astype(BF16)
    r = _sigmoid(_dot(xcb, wa_ref[...]) + ba_ref[...])
    ig = _sigmoid(_dot(xcb, wx_ref[...]) + bx_ref[...])
    log_a = (-RG_C * _softplus(-lam_ref[...])) * r
    a = jnp.exp(log_a)
    b = jnp.sqrt(_neg_expm1(2.0 * log_a)) * (ig * xc)

    row = lax.broadcasted_iota(jnp.int32, a.shape, 0)
    s = 1
    while s < tt:
        valid = row >= s
        b = b + a * jnp.where(valid, pltpu.roll(b, s, axis=0), 0.0)
        a = a * jnp.where(valid, pltpu.roll(a, s, axis=0), 1.0)
        s *= 2
    h = b + a * hc_scr[SUBLANES - 1:SUBLANES, :]
    hc_scr[...] = h[tt - SUBLANES:]
    rg_ref[0] = (h * _gelu_tanh(xg_ref[0])).astype(rg_ref.dtype)

    @pl.when(ti == pl.num_programs(1) - 1)
    def _():
        cnew_ref[0] = xcat[tt:]
        hlast_ref[0] = h[tt - SUBLANES:]


def _rg_call(xrg, conv_state8, h0_8, cw, cb, wa, ba, wx, bx, lam, tile=256):
    bsz, t_len, two_w = xrg.shape
    w = two_w // 2
    tt = _row_tile(t_len, tile)
    kern = functools.partial(_rg_kernel, tt=tt)
    state_spec = pl.BlockSpec((1, SUBLANES, w), lambda b, t: (b, 0, 0))
    state_shape = jax.ShapeDtypeStruct((bsz, SUBLANES, w), F32)
    return pl.pallas_call(
        kern, grid=(bsz, t_len // tt),
        in_specs=[pl.BlockSpec((1, tt, w), lambda b, t: (b, t, 0)),
                  pl.BlockSpec((1, tt, w), lambda b, t: (b, t, 1)),
                  state_spec, state_spec,
                  _const_spec(cw.shape), _const_spec(cb.shape), _const_spec(wa.shape), _const_spec(ba.shape),
                  _const_spec(wx.shape), _const_spec(bx.shape), _const_spec(lam.shape)],
        out_specs=[pl.BlockSpec((1, tt, w), lambda b, t: (b, t, 0)), state_spec, state_spec],
        out_shape=[jax.ShapeDtypeStruct((bsz, t_len, w), BF16), state_shape, state_shape],
        scratch_shapes=[pltpu.VMEM((SUBLANES, w), F32), pltpu.VMEM((SUBLANES, w), F32)],
        compiler_params=_params(("parallel", "arbitrary")), name="rglru",
    )(xrg, xrg, conv_state8, h0_8, cw, cb, wa, ba, wx, bx, lam)


def _convmod_kernel(h_ref, cst_ref, g_ref, w1_ref, b1_ref, dw_ref, dwb_ref, lng_ref, lnb_ref,
                    w2_ref, b2_ref, o_ref, cnew_ref, ucat_scr, conv_scr, *, tt, hist, taps):
    ti = pl.program_id(1)

    @pl.when(ti == 0)
    def _():
        ucat_scr[:hist, :] = cst_ref[0]

    x = h_ref[0]
    d = x.shape[1]
    hn = _rms(x, g_ref[...]).astype(BF16)
    ag = _dot(hn, w1_ref[...]) + b1_ref[...]
    ucat_scr[hist:, :] = ag[:, :d] * _sigmoid(ag[:, d:])

    for c in range(d // LANES):
        cols = slice(c * LANES, (c + 1) * LANES)
        w_rows = [dw_ref[k:k + 1, cols] for k in range(taps)]
        bias = dwb_ref[:, cols]
        for r0 in range(0, tt, CONV_ROW_CHUNK):
            win = ucat_scr[r0:r0 + hist + CONV_ROW_CHUNK, cols]
            acc = bias + w_rows[taps - 1] * win[hist:]
            for r in range(SUBLANES):
                base = win if r == 0 else pltpu.roll(win, r, axis=0)
                for m in range((taps - 1 - r) // SUBLANES + 1):
                    k = SUBLANES * m + r
                    if k > 0:
                        lo = hist - SUBLANES * m
                        acc = acc + w_rows[taps - 1 - k] * base[lo:lo + CONV_ROW_CHUNK]
            conv_scr[r0:r0 + CONV_ROW_CHUNK, cols] = acc
    new_hist = ucat_scr[tt:, :]
    ucat_scr[:hist, :] = new_hist

    cv = conv_scr[...]
    mu = jnp.mean(cv, axis=-1, keepdims=True)
    xc = cv - mu
    ln = xc * lax.rsqrt(jnp.mean(xc * xc, axis=-1, keepdims=True) + NORM_EPS) * lng_ref[...] + lnb_ref[...]
    act = (ln * _sigmoid(ln)).astype(BF16)
    o_ref[0] = x + _dot(act, w2_ref[...]) + b2_ref[...]

    @pl.when(ti == pl.num_programs(1) - 1)
    def _():
        cnew_ref[0] = new_hist


def _convmod_call(h, conv_state_pad, g, w1, b1, dw, dwb, lng, lnb, w2, b2, tile=256):
    bsz, t_len, d = h.shape
    hist = conv_state_pad.shape[1]
    taps = dw.shape[0]
    tt = _row_tile(t_len, tile)
    assert tt >= hist and hist % SUBLANES == 0 and taps - 1 <= hist
    kern = functools.partial(_convmod_kernel, tt=tt, hist=hist, taps=taps)
    state_spec = pl.BlockSpec((1, hist, d), lambda b, t: (b, 0, 0))
    consts = [g, w1, b1, dw, dwb, lng, lnb, w2, b2]
    return pl.pallas_call(
        kern, grid=(bsz, t_len // tt),
        in_specs=[pl.BlockSpec((1, tt, d), lambda b, t: (b, t, 0)), state_spec]
                 + [_const_spec(c.shape) for c in consts],
        out_specs=[pl.BlockSpec((1, tt, d), lambda b, t: (b, t, 0)), state_spec],
        out_shape=[jax.ShapeDtypeStruct((bsz, t_len, d), F32),
                   jax.ShapeDtypeStruct((bsz, hist, d), F32)],
        scratch_shapes=[pltpu.VMEM((hist + tt, d), F32), pltpu.VMEM((tt, d), F32)],
        compiler_params=_params(("parallel", "arbitrary")), name="convmod",
    )(h, conv_state_pad, *consts)


def _block_diag(w):
    g, i, j = w.shape
    eye = jnp.eye(g, dtype=w.dtype)
    return (eye[:, None, :, None] * w[:, :, None, :]).reshape(g * i, g * j)


def _pad_rows_front(x, rows):
    return jnp.pad(x, ((0, 0), (rows - x.shape[1], 0), (0, 0)))


def _trunk(x, p, w, sb_k, sb_v, rg_conv, rg_h, cm_conv):
    bsz, t_len, d = x.shape
    n = bsz * t_len
    fresh = sb_k is None
    depth = w['norm_ffn'].shape[0]
    row2 = lambda v: v.reshape(1, -1)
    bf = lambda v: v.astype(BF16)
    h = x.reshape(n, d)
    out_k, out_v, out_rc, out_rh, out_cc = [], [], [], [], []
    for i in range(depth):
        j = i // 2
        ffn = lambda hh, s, **kw: _ffn_call(
            hh, row2(w['norm_ffn'][i, s]), bf(w['w_ffn_gate'][i, s]), bf(w['w_ffn_up'][i, s]),
            bf(w['w_ffn_down'][i, s]), **kw)
        ple = (p[i].reshape(n, -1), row2(w['norm_ple'][i]), bf(w['w_ple_gate'][i]), bf(w['w_ple_proj'][i]))
        final_g = row2(w['norm_final']) if i == depth - 1 else None
        h = ffn(h, 0)
        if i % 2 == 0:
            rg_w = w['rg_lambda'].shape[1]
            sb_w = (w['w_ab_in'].shape[2] - 2 * rg_w) // 3
            qkv, k_new, v_new, xrg = _inproj_call(h, row2(w['norm_mix'][i]), bf(w['w_ab_in'][j]),
                                                  bsz, t_len, sb_w)
            qkv3 = qkv.reshape(bsz, t_len, 3 * sb_w)
            if fresh:
                sb = _attn_call(qkv3, 0, qkv3, 1, qkv3, 2, sb_w, 0, ATTN_KEY_BLOCK)
                conv0 = jnp.zeros((bsz, SUBLANES, rg_w), F32)
                h0 = jnp.zeros((bsz, SUBLANES, rg_w), F32)
            else:
                past = sb_k.shape[3]
                t_k = -(-(past + t_len) // ATTN_KEY_BLOCK) * ATTN_KEY_BLOCK

                def with_cache(cache, blk):
                    tm_major = cache.transpose(0, 2, 1, 3).reshape(bsz, past, sb_w).astype(BF16)
                    new = qkv3[:, :, blk * sb_w:(blk + 1) * sb_w]
                    full = jnp.concatenate([tm_major, new], axis=1)
                    return jnp.pad(full, ((0, 0), (0, t_k - past - t_len), (0, 0)))

                sb = _attn_call(qkv3, 0, with_cache(sb_k[j], 1), 0, with_cache(sb_v[j], 2), 0,
                                sb_w, past, t_len)
                conv0 = _pad_rows_front(rg_conv[j], SUBLANES)
                h0 = jnp.broadcast_to(rg_h[j][:, None, :], (bsz, SUBLANES, rg_w))
            rg, rc8, rh8 = _rg_call(
                xrg.reshape(bsz, t_len, 2 * rg_w), conv0, h0, w['rg_conv_w'][j], row2(w['rg_conv_b'][j]),
                bf(_block_diag(w['rg_gate_a_w'][j])), row2(w['rg_gate_a_b'][j]),
                bf(_block_diag(w['rg_gate_x_w'][j])), row2(w['rg_gate_x_b'][j]), row2(w['rg_lambda'][j]))
            out_k.append(k_new)
            out_v.append(v_new)
            out_rc.append(rc8[:, SUBLANES - (w['rg_conv_w'].shape[1] - 1):])
            out_rh.append(rh8[:, SUBLANES - 1])
            mix = (sb.reshape(n, sb_w), rg.reshape(n, rg_w), bf(w['w_ab_out'][j]))
            h = ffn(h, 1, mix=mix, ple=ple, final_g=final_g)
        else:
            taps = w['cm_dw_w'].shape[1]
            hist = -(-(taps - 1) // SUBLANES) * SUBLANES
            cc0 = jnp.zeros((bsz, hist, d), F32) if fresh else _pad_rows_front(cm_conv[j], hist)
            h3, cc_new = _convmod_call(
                h.reshape(bsz, t_len, d), cc0, row2(w['norm_mix'][i]), bf(w['cm_pw1_w'][j]),
                row2(w['cm_pw1_b'][j]), w['cm_dw_w'][j], row2(w['cm_dw_b'][j]), row2(w['cm_ln_g'][j]),
                row2(w['cm_ln_b'][j]), bf(w['cm_pw2_w'][j]), row2(w['cm_pw2_b'][j]))
            out_cc.append(cc_new[:, hist - (taps - 1):])
            h = ffn(h3.reshape(n, d), 1, ple=ple, final_g=final_g)
    y = h.reshape(bsz, t_len, d)
    return y, jnp.stack(out_k), jnp.stack(out_v), jnp.stack(out_rc), jnp.stack(out_rh), jnp.stack(out_cc)


def kernel(x_prompt, x_sample, cache_sb_k, cache_sb_v, state_rg_conv, state_rg_h, state_cm_conv,
           p_prompt, p_sample, norm_ffn, w_ffn_gate, w_ffn_up, w_ffn_down, norm_mix, w_ab_in, w_ab_out,
           rg_conv_w, rg_conv_b, rg_gate_a_w, rg_gate_a_b, rg_gate_x_w, rg_gate_x_b, rg_lambda,
           cm_pw1_w, cm_pw1_b, cm_dw_w, cm_dw_b, cm_ln_g, cm_ln_b, cm_pw2_w, cm_pw2_b,
           norm_ple, w_ple_gate, w_ple_proj, norm_final):
    w = dict(norm_ffn=norm_ffn, w_ffn_gate=w_ffn_gate, w_ffn_up=w_ffn_up, w_ffn_down=w_ffn_down,
             norm_mix=norm_mix, w_ab_in=w_ab_in, w_ab_out=w_ab_out, rg_conv_w=rg_conv_w, rg_conv_b=rg_conv_b,
             rg_gate_a_w=rg_gate_a_w, rg_gate_a_b=rg_gate_a_b, rg_gate_x_w=rg_gate_x_w, rg_gate_x_b=rg_gate_x_b,
             rg_lambda=rg_lambda, cm_pw1_w=cm_pw1_w, cm_pw1_b=cm_pw1_b, cm_dw_w=cm_dw_w, cm_dw_b=cm_dw_b,
             cm_ln_g=cm_ln_g, cm_ln_b=cm_ln_b, cm_pw2_w=cm_pw2_w, cm_pw2_b=cm_pw2_b,
             norm_ple=norm_ple, w_ple_gate=w_ple_gate, w_ple_proj=w_ple_proj, norm_final=norm_final)
    y_p, k_p, v_p, rc_p, rh_p, cc_p = _trunk(x_prompt, p_prompt, w, None, None, None, None, None)
    y_s, k_s, v_s, rc_s, rh_s, cc_s = _trunk(x_sample, p_sample, w, cache_sb_k, cache_sb_v,
                                             state_rg_conv, state_rg_h, state_cm_conv)
    return (y_p, y_s, k_p, v_p, rc_p, rh_p, cc_p, k_s, v_s, rc_s, rh_s, cc_s)
```

```python
import functools
import math

import jax
import jax.numpy as jnp
from jax import lax
from jax.experimental import pallas as pl
from jax.experimental.pallas import tpu as pltpu

NORM_EPS = 1e-6
SB_HEAD_DIM = 64
RG_C = 8.0
LANES = 128
SUBLANES = 8
MXU_DIM = 256
VMEM_LIMIT_BYTES = 56 * 1024 * 1024
EXP_UNDERFLOW = -105.0
FUSED_TILE = 512
CONV_ROW_CHUNK = 64

BF16 = jnp.bfloat16
F32 = jnp.float32


def _dot(a, b):
    return jnp.dot(a, b, preferred_element_type=F32)


def _dot_nt(a, b):
    return lax.dot_general(a, b, (((1,), (1,)), ((), ())), preferred_element_type=F32)


def _sigmoid(x):
    return 1.0 / (1.0 + jnp.exp(-x))


def _softplus(x):
    return jnp.maximum(x, 0.0) + jnp.log(1.0 + jnp.exp(-jnp.abs(x)))


def _log_sigmoids(z):
    nz = -z
    log_keep = jnp.minimum(nz, 0.0) - jnp.log(1.0 + jnp.exp(jnp.minimum(z, nz)))
    return log_keep, z + log_keep


def _neg_expm1(t):
    series = 1.0 / 40320.0
    for c in (1.0 / 5040.0, 1.0 / 720.0, 1.0 / 120.0, 1.0 / 24.0, 1.0 / 6.0, 0.5, 1.0):
        series = series * t + c
    return jnp.where(t > -0.3, -t * series, 1.0 - jnp.exp(t))


def _rms(x, g):
    return x * lax.rsqrt(jnp.mean(x * x, axis=-1, keepdims=True) + NORM_EPS) * g


def _gelu_tanh(x):
    return 0.5 * x * (1.0 + jnp.tanh(math.sqrt(2.0 / math.pi) * (x + 0.044715 * (x * x * x))))


def _const_spec(shape):
    zeros = (0,) * len(shape)
    return pl.BlockSpec(shape, lambda *_: zeros, pipeline_mode=pl.Buffered(1))


def _params(semantics):
    return pltpu.CompilerParams(dimension_semantics=semantics, vmem_limit_bytes=VMEM_LIMIT_BYTES)


def _row_tile(n, target):
    t = min(n, target)
    assert n % t == 0, (n, t)
    return t


def _ffn_tile(h, g_ref, wg_ref, wu_ref, wd_ref, ple, gf_ref, ff_splits, side_work=()):
    xn = _rms(h, g_ref[...]).astype(BF16)
    acc = None
    side_work = list(side_work)
    per_split = -(-len(side_work) // len(ff_splits))
    for s, e in ff_splits:
        gate = _dot(xn, wg_ref[:, s:e])
        up = _dot(xn, wu_ref[:, s:e])
        for work in side_work[:per_split]:
            work()
        side_work = side_work[per_split:]
        act = (gate * _sigmoid(gate) * up).astype(BF16)
        part = _dot(act, wd_ref[s:e, :])
        acc = part if acc is None else acc + part
    h = h + 0.5 * acc
    if ple is not None:
        p_ref, gp_ref, wpg_ref, wpp_ref = ple
        gate = _sigmoid(_dot(_rms(h, gp_ref[...]).astype(BF16), wpg_ref[...]))
        h = h + gate * _dot(p_ref[...].astype(BF16), wpp_ref[...])
    if gf_ref is not None:
        h = _rms(h, gf_ref[...])
    return h


def _ffn_kernel(*refs, has_mix, has_ple, has_final, ff_splits):
    it = iter(refs)
    h_ref = next(it)
    if has_mix:
        sb_ref, rg_ref, wo_ref = next(it), next(it), next(it)
    g_ref, wg_ref, wu_ref, wd_ref = next(it), next(it), next(it), next(it)
    ple = (next(it), next(it), next(it), next(it)) if has_ple else None
    gf_ref = next(it) if has_final else None
    o_ref = next(it)

    h = h_ref[...]
    if has_mix:
        half = sb_ref.shape[-1]
        h = h + _dot(sb_ref[...], wo_ref[:half, :]) + _dot(rg_ref[...], wo_ref[half:, :])
    o_ref[...] = _ffn_tile(h, g_ref, wg_ref, wu_ref, wd_ref, ple, gf_ref, ff_splits)


def _ff_splits(d_ff, n=2):
    if d_ff % MXU_DIM == 0 and d_ff // MXU_DIM >= n:
        tiles = d_ff // MXU_DIM
        edges = [(-(-tiles * k // n)) * MXU_DIM for k in range(n + 1)]
        return tuple(zip(edges[:-1], edges[1:]))
    return ((0, d_ff),)


def _ffn_call(h, g, wg, wu, wd, mix=None, ple=None, final_g=None, tile=512):
    n, d = h.shape
    d_ff = wg.shape[1]
    tm = _row_tile(n, tile)
    row = lambda w: pl.BlockSpec((tm, w), lambda i: (i, 0))
    args, specs = [h], [row(d)]
    if mix is not None:
        sb, rg, wo = mix
        args += [sb, rg, wo]
        specs += [row(sb.shape[1]), row(rg.shape[1]), _const_spec(wo.shape)]
    args += [g, wg, wu, wd]
    specs += [_const_spec(g.shape), _const_spec(wg.shape), _const_spec(wu.shape), _const_spec(wd.shape)]
    if ple is not None:
        p, gp, wpg, wpp = ple
        args += [p, gp, wpg, wpp]
        specs += [row(p.shape[1]), _const_spec(gp.shape), _const_spec(wpg.shape), _const_spec(wpp.shape)]
    if final_g is not None:
        args.append(final_g)
        specs.append(_const_spec(final_g.shape))
    kern = functools.partial(_ffn_kernel, has_mix=mix is not None, has_ple=ple is not None,
                             has_final=final_g is not None, ff_splits=_ff_splits(d_ff))
    return pl.pallas_call(
        kern, grid=(n // tm,), in_specs=specs, out_specs=row(d),
        out_shape=jax.ShapeDtypeStruct((n, d), F32),
        compiler_params=_params(("parallel",)), name="ffn",
    )(*args)


def _inproj_kernel(h_ref, g_ref, w_ref, qkv_ref, k_ref, v_ref, xrg_ref, *, sb_width, seq_rows):
    hn = _rms(h_ref[...], g_ref[...]).astype(BF16)
    u = _dot(hn, w_ref[...])
    scale = SB_HEAD_DIM ** -0.5
    qkv_ref[:, :sb_width] = (u[:, :sb_width] * scale).astype(BF16)
    qkv_ref[:, sb_width:] = u[:, sb_width:3 * sb_width].astype(BF16)
    xrg_ref[...] = u[:, 3 * sb_width:]
    n_seq, n_heads = k_ref.shape[0], k_ref.shape[1]
    for s in range(n_seq):
        rows = slice(s * seq_rows, (s + 1) * seq_rows)
        for hd in range(n_heads):
            c0 = sb_width + hd * SB_HEAD_DIM
            k_ref[s, hd] = u[rows, c0:c0 + SB_HEAD_DIM]
            v_ref[s, hd] = u[rows, c0 + sb_width:c0 + sb_width + SB_HEAD_DIM]


def _inproj_call(h, g, w_in, bsz, t_len, sb_width, tile=512):
    n, d = h.shape
    width = w_in.shape[1]
    n_heads = sb_width // SB_HEAD_DIM
    tm = _row_tile(n, tile)
    if tm <= t_len:
        assert t_len % tm == 0
        per = t_len // tm
        n_seq, seq_rows = 1, tm
        kv_map = lambda i: (i // per, 0, i % per, 0)
    else:
        assert tm % t_len == 0
        n_seq, seq_rows = tm // t_len, t_len
        kv_map = lambda i: (i, 0, 0, 0)
    kv_spec = pl.BlockSpec((n_seq, n_heads, seq_rows, SB_HEAD_DIM), kv_map)
    kv_shape = jax.ShapeDtypeStruct((bsz, n_heads, t_len, SB_HEAD_DIM), F32)
    row = lambda w: pl.BlockSpec((tm, w), lambda i: (i, 0))
    kern = functools.partial(_inproj_kernel, sb_width=sb_width, seq_rows=seq_rows)
    return pl.pallas_call(
        kern, grid=(n // tm,),
        in_specs=[row(d), _const_spec(g.shape), _const_spec(w_in.shape)],
        out_specs=[row(3 * sb_width), kv_spec, kv_spec, row(width - 3 * sb_width)],
        out_shape=[jax.ShapeDtypeStruct((n, 3 * sb_width), BF16), kv_shape, kv_shape,
                   jax.ShapeDtypeStruct((n, width - 3 * sb_width), F32)],
        compiler_params=_params(("parallel",)), name="inproj",
    )(h, g, w_in)


ATTN_KEY_BLOCK = LANES
ATTN_WINDOW = 3 * ATTN_KEY_BLOCK


def _attn_kernel(q_ref, k_ref, v_ref, o_ref, *, tq, offset):
    tk = ATTN_KEY_BLOCK
    n_sub = ATTN_WINDOW // tk
    n_pairs = q_ref.shape[2] // LANES
    i = pl.program_id(1)
    q0 = offset + i * tq
    w_end = (offset + tq + tk - 1) // tk * tk + i * tq
    w_start = pl.multiple_of(jnp.maximum(w_end - ATTN_WINDOW, 0), tk)

    zero = jnp.zeros((), BF16)
    first_q = lax.broadcasted_iota(jnp.int32, (tq, LANES), 1) < SB_HEAD_DIM
    r_i = lax.broadcasted_iota(jnp.int32, (tk, tk), 0)
    c_i = lax.broadcasted_iota(jnp.int32, (tk, tk), 1)
    tri = jnp.where(r_i > c_i, 1.0, 0.0).astype(BF16)
    sum_mat = jnp.concatenate([tri, jnp.ones((tk, tk), BF16)], axis=1)
    sum_mat = jnp.concatenate([sum_mat, sum_mat], axis=0)

    def block_sums(log_keep):
        hi = log_keep.astype(BF16)
        lo = (log_keep - hi.astype(F32)).astype(BF16)
        sums = _dot(jnp.concatenate([hi, lo], axis=1), sum_mat)
        return sums[:, :tk], sums[:, tk:]

    def q_heads(pair):
        q2 = q_ref[0, :, pair * LANES:(pair + 1) * LANES]
        return jnp.where(first_q, q2, zero), jnp.where(first_q, zero, q2)

    def v_heads(vb):
        first_k = lax.broadcasted_iota(jnp.int32, vb.shape, 1) < SB_HEAD_DIM
        return jnp.concatenate([jnp.where(first_k, vb, zero), jnp.where(first_k, zero, vb)], axis=0)

    col_minus_row = (lax.broadcasted_iota(jnp.int32, (tq, ATTN_WINDOW), 1)
                     - lax.broadcasted_iota(jnp.int32, (tq, ATTN_WINDOW), 0))
    mask = col_minus_row < (q0 - w_start)
    n_heads = 2 * n_pairs
    scores = []
    for pair in range(n_pairs):
        kw = k_ref[0, pl.ds(w_start, ATTN_WINDOW), pair * LANES:(pair + 1) * LANES]
        scores += [_dot_nt(qh, kw) for qh in q_heads(pair)]
    log_betas, splits = [], []
    for z in scores:
        log_keep, log_beta = _log_sigmoids(z)
        log_keep = jnp.where(mask, log_keep, 0.0)
        hi = log_keep.astype(BF16)
        lo = (log_keep - hi.astype(F32)).astype(BF16)
        log_betas.append(log_beta)
        splits.append((hi, lo))
    sums = [[_dot(jnp.concatenate([hi[:, sub * tk:(sub + 1) * tk], lo[:, sub * tk:(sub + 1) * tk]], axis=1),
                  sum_mat) for sub in range(n_sub)] for hi, lo in splits]
    carries, probs = [], []
    for hd in range(n_heads):
        later, carry = [None] * n_sub, None
        for sub in reversed(range(n_sub)):
            local, total = sums[hd][sub][:, :tk], sums[hd][sub][:, tk:]
            later[sub] = local if carry is None else local + carry
            carry = total if carry is None else carry + total
        a = jnp.where(mask, jnp.exp(log_betas[hd] + jnp.concatenate(later, axis=1)), 0.0)
        probs.append(a.astype(BF16))
        carries.append(carry)
    accs = []
    for pair in range(n_pairs):
        vw = v_ref[0, pl.ds(w_start, ATTN_WINDOW), pair * LANES:(pair + 1) * LANES]
        accs.append(_dot(jnp.concatenate(probs[2 * pair:2 * pair + 2], axis=1), v_heads(vw)))

    def live(cs):
        m = cs[0]
        for c in cs[1:]:
            m = jnp.maximum(m, c)
        return (jnp.max(m) >= EXP_UNDERFLOW).astype(jnp.int32)

    def cond(state):
        return jnp.logical_and(state[0] >= 0, state[1] > 0)

    def body(state):
        j = state[0]
        cs, acs = list(state[2]), list(state[3])
        start = pl.multiple_of(j * tk, tk)
        for pair in range(n_pairs):
            cols = slice(pair * LANES, (pair + 1) * LANES)
            kb = k_ref[0, pl.ds(start, tk), cols]
            vb = v_ref[0, pl.ds(start, tk), cols]
            probs = []
            for hd, qh in enumerate(q_heads(pair)):
                z = _dot_nt(qh, kb)
                log_keep, log_beta = _log_sigmoids(z)
                local, total = block_sums(log_keep)
                carry = cs[2 * pair + hd]
                probs.append(jnp.exp(log_beta + local + carry).astype(BF16))
                cs[2 * pair + hd] = carry + total
            acs[pair] = acs[pair] + _dot(jnp.concatenate(probs, axis=1), v_heads(vb))
        return j - 1, live(cs), tuple(cs), tuple(acs)

    state = lax.while_loop(cond, body, (w_start // tk - 1, live(carries), tuple(carries), tuple(accs)))
    for pair in range(n_pairs):
        o_ref[0, :, pair * LANES:(pair + 1) * LANES] = state[3][pair].astype(o_ref.dtype)


def _attn_call(q_arr, q_blk, k_arr, k_blk, v_arr, v_blk, sb_width, offset, tq):
    bsz, t_q, _ = q_arr.shape
    t_k = k_arr.shape[1]
    assert t_q % tq == 0 and offset % tq == 0 and (tq == ATTN_KEY_BLOCK or tq == t_q)
    assert t_k % ATTN_KEY_BLOCK == 0 and t_k >= ATTN_WINDOW and offset + t_q <= t_k
    kern = functools.partial(_attn_kernel, tq=tq, offset=offset)
    kv_spec = lambda blk: pl.BlockSpec((1, t_k, sb_width), lambda b, i: (b, 0, blk),
                                       pipeline_mode=pl.Buffered(1))
    return pl.pallas_call(
        kern, grid=(bsz, t_q // tq),
        in_specs=[pl.BlockSpec((1, tq, sb_width), lambda b, i: (b, i, q_blk)), kv_spec(k_blk), kv_spec(v_blk)],
        out_specs=pl.BlockSpec((1, tq, sb_width), lambda b, i: (b, i, 0)),
        out_shape=jax.ShapeDtypeStruct((bsz, t_q, sb_width), BF16),
        compiler_params=_params(("parallel", "arbitrary")), name="stickbreak",
    )(q_arr, k_arr, v_arr)


def _rg_kernel(xr_ref, xg_ref, cst_ref, h0_ref, cw_ref, cb_ref, wa_ref, ba_ref, wx_ref, bx_ref,
               lam_ref, rg_ref, cnew_ref, hlast_ref, hist_scr, hc_scr, *, tt):
    ti = pl.program_id(1)

    @pl.when(ti == 0)
    def _():
        hist_scr[...] = cst_ref[0]
        hc_scr[...] = h0_ref[0]

    x = xr_ref[0]
    conv_w = cw_ref.shape[0]
    xcat = jnp.concatenate([hist_scr[...], x], axis=0)
    xc = cb_ref[...] + cw_ref[conv_w - 1:conv_w, :] * x
    for k in range(1, conv_w):
        xc = xc + cw_ref[conv_w - 1 - k:conv_w - k, :] * pltpu.roll(xcat, k, axis=0)[SUBLANES:]
    hist_scr[...] = xcat[tt:]

    xcb = xc.astype(BF16)
    r = _sigmoid(_dot(xcb, wa_ref[...]) + ba_ref[...])
    ig = _sigmoid(_dot(xcb, wx_ref[...]) + bx_ref[...])
    log_a = (-RG_C * _softplus(-lam_ref[...])) * r
    a = jnp.exp(log_a)
    b = jnp.sqrt(_neg_expm1(2.0 * log_a)) * (ig * xc)

    row = lax.broadcasted_iota(jnp.int32, a.shape, 0)
    s = 1
    while s < tt:
        valid = row >= s
        b = b + a * jnp.where(valid, pltpu.roll(b, s, axis=0), 0.0)
        a = a * jnp.where(valid, pltpu.roll(a, s, axis=0), 1.0)
        s *= 2
    h = b + a * hc_scr[SUBLANES - 1:SUBLANES, :]
    hc_scr[...] = h[tt - SUBLANES:]
    rg_ref[0] = (h * _gelu_tanh(xg_ref[0])).astype(rg_ref.dtype)

    @pl.when(ti == pl.num_programs(1) - 1)
    def _():
        cnew_ref[0] = xcat[tt:]
        hlast_ref[0] = h[tt - SUBLANES:]


def _rg_call(xrg, conv_state8, h0_8, cw, cb, wa, ba, wx, bx, lam, tile=256):
    bsz, t_len, two_w = xrg.shape
    w = two_w // 2
    tt = _row_tile(t_len, tile)
    kern = functools.partial(_rg_kernel, tt=tt)
    state_spec = pl.BlockSpec((1, SUBLANES, w), lambda b, t: (b, 0, 0))
    state_shape = jax.ShapeDtypeStruct((bsz, SUBLANES, w), F32)
    return pl.pallas_call(
        kern, grid=(bsz, t_len // tt),
        in_specs=[pl.BlockSpec((1, tt, w), lambda b, t: (b, t, 0)),
                  pl.BlockSpec((1, tt, w), lambda b, t: (b, t, 1)),
                  state_spec, state_spec,
                  _const_spec(cw.shape), _const_spec(cb.shape), _const_spec(wa.shape), _const_spec(ba.shape),
                  _const_spec(wx.shape), _const_spec(bx.shape), _const_spec(lam.shape)],
        out_specs=[pl.BlockSpec((1, tt, w), lambda b, t: (b, t, 0)), state_spec, state_spec],
        out_shape=[jax.ShapeDtypeStruct((bsz, t_len, w), BF16), state_shape, state_shape],
        scratch_shapes=[pltpu.VMEM((SUBLANES, w), F32), pltpu.VMEM((SUBLANES, w), F32)],
        compiler_params=_params(("parallel", "arbitrary")), name="rglru",
    )(xrg, xrg, conv_state8, h0_8, cw, cb, wa, ba, wx, bx, lam)


def _convmod_glu(x, g_ref, w1_ref, b1_ref, ucat_scr, hist):
    d = x.shape[1]
    hn = _rms(x, g_ref[...]).astype(BF16)
    ag = _dot(hn, w1_ref[...]) + b1_ref[...]
    ucat_scr[hist:, :] = ag[:, :d] * _sigmoid(ag[:, d:])


def _conv_chunk(c, dw_ref, dwb_ref, ucat_scr, conv_scr, tt, hist, taps):
    cols = slice(c * LANES, (c + 1) * LANES)
    w_rows = [dw_ref[k:k + 1, cols] for k in range(taps)]
    bias = dwb_ref[:, cols]
    for r0 in range(0, tt, CONV_ROW_CHUNK):
        win = ucat_scr[r0:r0 + hist + CONV_ROW_CHUNK, cols]
        acc = bias + w_rows[taps - 1] * win[hist:]
        for r in range(SUBLANES):
            base = win if r == 0 else pltpu.roll(win, r, axis=0)
            for m in range((taps - 1 - r) // SUBLANES + 1):
                k = SUBLANES * m + r
                if k > 0:
                    lo = hist - SUBLANES * m
                    acc = acc + w_rows[taps - 1 - k] * base[lo:lo + CONV_ROW_CHUNK]
        conv_scr[r0:r0 + CONV_ROW_CHUNK, cols] = acc


def _convmod_out(x, lng_ref, lnb_ref, w2_ref, b2_ref, ucat_scr, conv_scr, tt, hist):
    new_hist = ucat_scr[tt:, :]
    ucat_scr[:hist, :] = new_hist
    cv = conv_scr[...]
    mu = jnp.mean(cv, axis=-1, keepdims=True)
    xc = cv - mu
    ln = xc * lax.rsqrt(jnp.mean(xc * xc, axis=-1, keepdims=True) + NORM_EPS) * lng_ref[...] + lnb_ref[...]
    act = (ln * _sigmoid(ln)).astype(BF16)
    return x + _dot(act, w2_ref[...]) + b2_ref[...], new_hist


def _convmod_tile(x, g_ref, w1_ref, b1_ref, dw_ref, dwb_ref, lng_ref, lnb_ref, w2_ref, b2_ref,
                  ucat_scr, conv_scr, *, tt, hist, taps):
    _convmod_glu(x, g_ref, w1_ref, b1_ref, ucat_scr, hist)
    for c in range(x.shape[1] // LANES):
        _conv_chunk(c, dw_ref, dwb_ref, ucat_scr, conv_scr, tt, hist, taps)
    return _convmod_out(x, lng_ref, lnb_ref, w2_ref, b2_ref, ucat_scr, conv_scr, tt, hist)


def _convmod_kernel(h_ref, cst_ref, *refs, tt, hist, taps):
    consts, (o_ref, cnew_ref, ucat_scr, conv_scr) = refs[:9], refs[9:]
    ti = pl.program_id(1)

    @pl.when(ti == 0)
    def _():
        ucat_scr[:hist, :] = cst_ref[0]

    o_ref[0], new_hist = _convmod_tile(h_ref[0], *consts, ucat_scr, conv_scr, tt=tt, hist=hist, taps=taps)

    @pl.when(ti == pl.num_programs(1) - 1)
    def _():
        cnew_ref[0] = new_hist


def _convmod_call(h, conv_state_pad, g, w1, b1, dw, dwb, lng, lnb, w2, b2, tile=256):
    bsz, t_len, d = h.shape
    hist = conv_state_pad.shape[1]
    taps = dw.shape[0]
    tt = _row_tile(t_len, tile)
    assert tt >= hist and hist % SUBLANES == 0 and taps - 1 <= hist
    kern = functools.partial(_convmod_kernel, tt=tt, hist=hist, taps=taps)
    state_spec = pl.BlockSpec((1, hist, d), lambda b, t: (b, 0, 0))
    consts = [g, w1, b1, dw, dwb, lng, lnb, w2, b2]
    return pl.pallas_call(
        kern, grid=(bsz, t_len // tt),
        in_specs=[pl.BlockSpec((1, tt, d), lambda b, t: (b, t, 0)), state_spec]
                 + [_const_spec(c.shape) for c in consts],
        out_specs=[pl.BlockSpec((1, tt, d), lambda b, t: (b, t, 0)), state_spec],
        out_shape=[jax.ShapeDtypeStruct((bsz, t_len, d), F32),
                   jax.ShapeDtypeStruct((bsz, hist, d), F32)],
        scratch_shapes=[pltpu.VMEM((hist + tt, d), F32), pltpu.VMEM((tt, d), F32)],
        compiler_params=_params(("parallel", "arbitrary")), name="convmod",
    )(h, conv_state_pad, *consts)


def _convffn_kernel(h_ref, cst_ref, *refs, tt, hist, taps, tiles_per_seq, has_final, ff_splits):
    cm_consts, refs = refs[:9], refs[9:]
    g_ref, wg_ref, wu_ref, wd_ref = refs[:4]
    ple = refs[4:8]
    gf_ref = refs[8] if has_final else None
    o_ref, cnew_ref, ucat_scr, conv_scr, h_scr = refs[8 + has_final:]
    s = pl.program_id(0)
    n_tiles = pl.num_programs(0) - 1
    ti = lax.rem(jnp.minimum(s, n_tiles - 1), tiles_per_seq)

    @pl.when(s == 0)
    def _():
        h_scr[...] = jnp.zeros_like(h_scr)

    @pl.when(ti == 0)
    def _():
        ucat_scr[:hist, :] = cst_ref[0]

    g_cm, w1_ref, b1_ref, dw_ref, dwb_ref, lng_ref, lnb_ref, w2_ref, b2_ref = cm_consts
    h_prev = h_scr[...]
    x = h_ref[...]
    _convmod_glu(x, g_cm, w1_ref, b1_ref, ucat_scr, hist)
    conv_chunks = [functools.partial(_conv_chunk, c, dw_ref, dwb_ref, ucat_scr, conv_scr, tt, hist, taps)
                   for c in range(x.shape[1] // LANES)]
    o_ref[...] = _ffn_tile(h_prev, g_ref, wg_ref, wu_ref, wd_ref, ple, gf_ref, ff_splits, side_work=conv_chunks)
    h_new, new_hist = _convmod_out(x, lng_ref, lnb_ref, w2_ref, b2_ref, ucat_scr, conv_scr, tt, hist)
    h_scr[...] = h_new

    @pl.when(jnp.logical_and(ti == tiles_per_seq - 1, s < n_tiles))
    def _():
        cnew_ref[0] = new_hist


def _convffn_call(h, t_len, conv_state_pad, cm_consts, g, wg, wu, wd, ple, final_g, tile):
    n, d = h.shape
    hist = conv_state_pad.shape[1]
    taps = cm_consts[3].shape[0]
    tt = tile
    assert t_len % tt == 0 and tt >= hist and hist % SUBLANES == 0 and taps - 1 <= hist
    tiles_per_seq, n_tiles = t_len // tt, n // tt
    p, gp, wpg, wpp = ple
    conv_tile = lambda s: jnp.minimum(s, n_tiles - 1)
    ffn_tile = lambda s: jnp.maximum(s - 1, 0)
    state_spec = pl.BlockSpec((1, hist, d), lambda s: (conv_tile(s) // tiles_per_seq, 0, 0))
    consts = list(cm_consts) + [g, wg, wu, wd]
    tail = [gp, wpg, wpp] + ([final_g] if final_g is not None else [])
    kern = functools.partial(_convffn_kernel, tt=tt, hist=hist, taps=taps, tiles_per_seq=tiles_per_seq,
                             has_final=final_g is not None, ff_splits=_ff_splits(wg.shape[1], 4))
    return pl.pallas_call(
        kern, grid=(n_tiles + 1,),
        in_specs=[pl.BlockSpec((tt, d), lambda s: (conv_tile(s), 0)), state_spec]
                 + [_const_spec(c.shape) for c in consts]
                 + [pl.BlockSpec((tt, p.shape[1]), lambda s: (ffn_tile(s), 0))]
                 + [_const_spec(c.shape) for c in tail],
        out_specs=[pl.BlockSpec((tt, d), lambda s: (ffn_tile(s), 0)), state_spec],
        out_shape=[jax.ShapeDtypeStruct((n, d), F32),
                   jax.ShapeDtypeStruct((conv_state_pad.shape[0], hist, d), F32)],
        scratch_shapes=[pltpu.VMEM((hist + tt, d), F32), pltpu.VMEM((tt, d), F32), pltpu.VMEM((tt, d), F32)],
        compiler_params=_params(("arbitrary",)), name="convffn",
    )(h, conv_state_pad, *consts, p, *tail)


def _block_diag(w):
    g, i, j = w.shape
    eye = jnp.eye(g, dtype=w.dtype)
    return (eye[:, None, :, None] * w[:, :, None, :]).reshape(g * i, g * j)


def _pad_rows_front(x, rows):
    return jnp.pad(x, ((0, 0), (rows - x.shape[1], 0), (0, 0)))


def _trunk(x, p, w, sb_k, sb_v, rg_conv, rg_h, cm_conv):
    bsz, t_len, d = x.shape
    n = bsz * t_len
    fresh = sb_k is None
    depth = w['norm_ffn'].shape[0]
    row2 = lambda v: v.reshape(1, -1)
    bf = lambda v: v.astype(BF16)
    h = x.reshape(n, d)
    out_k, out_v, out_rc, out_rh, out_cc = [], [], [], [], []
    for i in range(depth):
        j = i // 2
        ffn = lambda hh, s, **kw: _ffn_call(
            hh, row2(w['norm_ffn'][i, s]), bf(w['w_ffn_gate'][i, s]), bf(w['w_ffn_up'][i, s]),
            bf(w['w_ffn_down'][i, s]), **kw)
        ple = (p[i].reshape(n, -1), row2(w['norm_ple'][i]), bf(w['w_ple_gate'][i]), bf(w['w_ple_proj'][i]))
        final_g = row2(w['norm_final']) if i == depth - 1 else None
        h = ffn(h, 0)
        if i % 2 == 0:
            rg_w = w['rg_lambda'].shape[1]
            sb_w = (w['w_ab_in'].shape[2] - 2 * rg_w) // 3
            qkv, k_new, v_new, xrg = _inproj_call(h, row2(w['norm_mix'][i]), bf(w['w_ab_in'][j]),
                                                  bsz, t_len, sb_w)
            qkv3 = qkv.reshape(bsz, t_len, 3 * sb_w)
            if fresh:
                sb = _attn_call(qkv3, 0, qkv3, 1, qkv3, 2, sb_w, 0, ATTN_KEY_BLOCK)
                conv0 = jnp.zeros((bsz, SUBLANES, rg_w), F32)
                h0 = jnp.zeros((bsz, SUBLANES, rg_w), F32)
            else:
                past = sb_k.shape[3]
                t_k = -(-(past + t_len) // ATTN_KEY_BLOCK) * ATTN_KEY_BLOCK

                def with_cache(cache, blk):
                    tm_major = cache.transpose(0, 2, 1, 3).reshape(bsz, past, sb_w).astype(BF16)
                    new = qkv3[:, :, blk * sb_w:(blk + 1) * sb_w]
                    full = jnp.concatenate([tm_major, new], axis=1)
                    return jnp.pad(full, ((0, 0), (0, t_k - past - t_len), (0, 0)))

                sb = _attn_call(qkv3, 0, with_cache(sb_k[j], 1), 0, with_cache(sb_v[j], 2), 0,
                                sb_w, past, t_len)
                conv0 = _pad_rows_front(rg_conv[j], SUBLANES)
                h0 = jnp.broadcast_to(rg_h[j][:, None, :], (bsz, SUBLANES, rg_w))
            rg, rc8, rh8 = _rg_call(
                xrg.reshape(bsz, t_len, 2 * rg_w), conv0, h0, w['rg_conv_w'][j], row2(w['rg_conv_b'][j]),
                bf(_block_diag(w['rg_gate_a_w'][j])), row2(w['rg_gate_a_b'][j]),
                bf(_block_diag(w['rg_gate_x_w'][j])), row2(w['rg_gate_x_b'][j]), row2(w['rg_lambda'][j]))
            out_k.append(k_new)
            out_v.append(v_new)
            out_rc.append(rc8[:, SUBLANES - (w['rg_conv_w'].shape[1] - 1):])
            out_rh.append(rh8[:, SUBLANES - 1])
            mix = (sb.reshape(n, sb_w), rg.reshape(n, rg_w), bf(w['w_ab_out'][j]))
            h = ffn(h, 1, mix=mix, ple=ple, final_g=final_g)
        else:
            taps = w['cm_dw_w'].shape[1]
            hist = -(-(taps - 1) // SUBLANES) * SUBLANES
            cc0 = jnp.zeros((bsz, hist, d), F32) if fresh else _pad_rows_front(cm_conv[j], hist)
            cm_consts = (row2(w['norm_mix'][i]), bf(w['cm_pw1_w'][j]), row2(w['cm_pw1_b'][j]), w['cm_dw_w'][j],
                         row2(w['cm_dw_b'][j]), row2(w['cm_ln_g'][j]), row2(w['cm_ln_b'][j]),
                         bf(w['cm_pw2_w'][j]), row2(w['cm_pw2_b'][j]))
            if t_len >= FUSED_TILE:
                h, cc_new = _convffn_call(
                    h, t_len, cc0, cm_consts, row2(w['norm_ffn'][i, 1]), bf(w['w_ffn_gate'][i, 1]),
                    bf(w['w_ffn_up'][i, 1]), bf(w['w_ffn_down'][i, 1]), ple, final_g, FUSED_TILE)
            else:
                h3, cc_new = _convmod_call(h.reshape(bsz, t_len, d), cc0, *cm_consts)
                h = ffn(h3.reshape(n, d), 1, ple=ple, final_g=final_g)
            out_cc.append(cc_new[:, hist - (taps - 1):])
    y = h.reshape(bsz, t_len, d)
    return y, jnp.stack(out_k), jnp.stack(out_v), jnp.stack(out_rc), jnp.stack(out_rh), jnp.stack(out_cc)


def kernel(x_prompt, x_sample, cache_sb_k, cache_sb_v, state_rg_conv, state_rg_h, state_cm_conv,
           p_prompt, p_sample, norm_ffn, w_ffn_gate, w_ffn_up, w_ffn_down, norm_mix, w_ab_in, w_ab_out,
           rg_conv_w, rg_conv_b, rg_gate_a_w, rg_gate_a_b, rg_gate_x_w, rg_gate_x_b, rg_lambda,
           cm_pw1_w, cm_pw1_b, cm_dw_w, cm_dw_b, cm_ln_g, cm_ln_b, cm_pw2_w, cm_pw2_b,
           norm_ple, w_ple_gate, w_ple_proj, norm_final):
    w = dict(norm_ffn=norm_ffn, w_ffn_gate=w_ffn_gate, w_ffn_up=w_ffn_up, w_ffn_down=w_ffn_down,
             norm_mix=norm_mix, w_ab_in=w_ab_in, w_ab_out=w_ab_out, rg_conv_w=rg_conv_w, rg_conv_b=rg_conv_b,
             rg_gate_a_w=rg_gate_a_w, rg_gate_a_b=rg_gate_a_b, rg_gate_x_w=rg_gate_x_w, rg_gate_x_b=rg_gate_x_b,
             rg_lambda=rg_lambda, cm_pw1_w=cm_pw1_w, cm_pw1_b=cm_pw1_b, cm_dw_w=cm_dw_w, cm_dw_b=cm_dw_b,
             cm_ln_g=cm_ln_g, cm_ln_b=cm_ln_b, cm_pw2_w=cm_pw2_w, cm_pw2_b=cm_pw2_b,
             norm_ple=norm_ple, w_ple_gate=w_ple_gate, w_ple_proj=w_ple_proj, norm_final=norm_final)
    y_p, k_p, v_p, rc_p, rh_p, cc_p = _trunk(x_prompt, p_prompt, w, None, None, None, None, None)
    y_s, k_s, v_s, rc_s, rh_s, cc_s = _trunk(x_sample, p_sample, w, cache_sb_k, cache_sb_v,
                                             state_rg_conv, state_rg_h, state_cm_conv)
    return (y_p, y_s, k_p, v_p, rc_p, rh_p, cc_p, k_s, v_s, rc_s, rh_s, cc_s)
```

```python
import functools
import math

import jax
import jax.numpy as jnp
from jax import lax
from jax.experimental import pallas as pl
from jax.experimental.pallas import tpu as pltpu

NORM_EPS = 1e-6
SB_HEAD_DIM = 64
RG_C = 8.0
LANES = 128
SUBLANES = 8
MXU_DIM = 256
VMEM_LIMIT_BYTES = 56 * 1024 * 1024
EXP_UNDERFLOW = -105.0
FUSED_TILE = 512
CONV_ROW_CHUNK = 64

BF16 = jnp.bfloat16
F32 = jnp.float32


def _dot(a, b):
    return jnp.dot(a, b, preferred_element_type=F32)


def _dot_nt(a, b):
    return lax.dot_general(a, b, (((1,), (1,)), ((), ())), preferred_element_type=F32)


def _sigmoid(x):
    return 1.0 / (1.0 + jnp.exp(-x))


def _softplus(x):
    return jnp.maximum(x, 0.0) + jnp.log(1.0 + jnp.exp(-jnp.abs(x)))


def _log_sigmoids(z):
    nz = -z
    log_keep = jnp.minimum(nz, 0.0) - jnp.log(1.0 + jnp.exp(jnp.minimum(z, nz)))
    return log_keep, z + log_keep


def _neg_expm1(t):
    series = 1.0 / 40320.0
    for c in (1.0 / 5040.0, 1.0 / 720.0, 1.0 / 120.0, 1.0 / 24.0, 1.0 / 6.0, 0.5, 1.0):
        series = series * t + c
    return jnp.where(t > -0.3, -t * series, 1.0 - jnp.exp(t))


def _rms(x, g):
    return x * lax.rsqrt(jnp.mean(x * x, axis=-1, keepdims=True) + NORM_EPS) * g


def _gelu_tanh(x):
    return 0.5 * x * (1.0 + jnp.tanh(math.sqrt(2.0 / math.pi) * (x + 0.044715 * (x * x * x))))


def _const_spec(shape):
    zeros = (0,) * len(shape)
    return pl.BlockSpec(shape, lambda *_: zeros, pipeline_mode=pl.Buffered(1))


def _params(semantics):
    return pltpu.CompilerParams(dimension_semantics=semantics, vmem_limit_bytes=VMEM_LIMIT_BYTES)


def _row_tile(n, target):
    t = min(n, target)
    assert n % t == 0, (n, t)
    return t


def _ffn_tile(h, g_ref, wg_ref, wu_ref, wd_ref, ple, gf_ref, ff_splits, side_work=()):
    xn = _rms(h, g_ref[...]).astype(BF16)
    acc = None
    side_work = list(side_work)
    per_split = -(-len(side_work) // len(ff_splits))
    for s, e in ff_splits:
        gate = _dot(xn, wg_ref[:, s:e])
        up = _dot(xn, wu_ref[:, s:e])
        act = (gate * _sigmoid(gate) * up).astype(BF16)
        part = _dot(act, wd_ref[s:e, :])
        acc = part if acc is None else acc + part
        for work in side_work[:per_split]:
            work()
        side_work = side_work[per_split:]
    h = h + 0.5 * acc
    if ple is not None:
        p_ref, gp_ref, wpg_ref, wpp_ref = ple
        gate = _sigmoid(_dot(_rms(h, gp_ref[...]).astype(BF16), wpg_ref[...]))
        h = h + gate * _dot(p_ref[...].astype(BF16), wpp_ref[...])
    if gf_ref is not None:
        h = _rms(h, gf_ref[...])
    return h


def _ffn_kernel(*refs, has_mix, has_ple, has_final, ff_splits):
    it = iter(refs)
    h_ref = next(it)
    if has_mix:
        sb_ref, rg_ref, wo_ref = next(it), next(it), next(it)
    g_ref, wg_ref, wu_ref, wd_ref = next(it), next(it), next(it), next(it)
    ple = (next(it), next(it), next(it), next(it)) if has_ple else None
    gf_ref = next(it) if has_final else None
    o_ref = next(it)

    h = h_ref[...]
    if has_mix:
        half = sb_ref.shape[-1]
        h = h + _dot(sb_ref[...], wo_ref[:half, :]) + _dot(rg_ref[...], wo_ref[half:, :])
    o_ref[...] = _ffn_tile(h, g_ref, wg_ref, wu_ref, wd_ref, ple, gf_ref, ff_splits)


def _ff_splits(d_ff, n=2):
    if d_ff % MXU_DIM == 0 and d_ff // MXU_DIM >= n:
        tiles = d_ff // MXU_DIM
        edges = [(-(-tiles * k // n)) * MXU_DIM for k in range(n + 1)]
        return tuple(zip(edges[:-1], edges[1:]))
    return ((0, d_ff),)


def _ffn_call(h, g, wg, wu, wd, mix=None, ple=None, final_g=None, tile=512):
    n, d = h.shape
    d_ff = wg.shape[1]
    tm = _row_tile(n, tile)
    row = lambda w: pl.BlockSpec((tm, w), lambda i: (i, 0))
    args, specs = [h], [row(d)]
    if mix is not None:
        sb, rg, wo = mix
        args += [sb, rg, wo]
        specs += [row(sb.shape[1]), row(rg.shape[1]), _const_spec(wo.shape)]
    args += [g, wg, wu, wd]
    specs += [_const_spec(g.shape), _const_spec(wg.shape), _const_spec(wu.shape), _const_spec(wd.shape)]
    if ple is not None:
        p, gp, wpg, wpp = ple
        args += [p, gp, wpg, wpp]
        specs += [row(p.shape[1]), _const_spec(gp.shape), _const_spec(wpg.shape), _const_spec(wpp.shape)]
    if final_g is not None:
        args.append(final_g)
        specs.append(_const_spec(final_g.shape))
    kern = functools.partial(_ffn_kernel, has_mix=mix is not None, has_ple=ple is not None,
                             has_final=final_g is not None, ff_splits=_ff_splits(d_ff))
    return pl.pallas_call(
        kern, grid=(n // tm,), in_specs=specs, out_specs=row(d),
        out_shape=jax.ShapeDtypeStruct((n, d), F32),
        compiler_params=_params(("parallel",)), name="ffn",
    )(*args)


def _inproj_kernel(h_ref, g_ref, w_ref, qkv_ref, k_ref, v_ref, xrg_ref, *, sb_width, seq_rows):
    hn = _rms(h_ref[...], g_ref[...]).astype(BF16)
    u = _dot(hn, w_ref[...])
    scale = SB_HEAD_DIM ** -0.5
    qkv_ref[:, :sb_width] = (u[:, :sb_width] * scale).astype(BF16)
    qkv_ref[:, sb_width:] = u[:, sb_width:3 * sb_width].astype(BF16)
    xrg_ref[...] = u[:, 3 * sb_width:]
    n_seq, n_heads = k_ref.shape[0], k_ref.shape[1]
    for s in range(n_seq):
        rows = slice(s * seq_rows, (s + 1) * seq_rows)
        for hd in range(n_heads):
            c0 = sb_width + hd * SB_HEAD_DIM
            k_ref[s, hd] = u[rows, c0:c0 + SB_HEAD_DIM]
            v_ref[s, hd] = u[rows, c0 + sb_width:c0 + sb_width + SB_HEAD_DIM]


def _inproj_call(h, g, w_in, bsz, t_len, sb_width, tile=512):
    n, d = h.shape
    width = w_in.shape[1]
    n_heads = sb_width // SB_HEAD_DIM
    tm = _row_tile(n, tile)
    if tm <= t_len:
        assert t_len % tm == 0
        per = t_len // tm
        n_seq, seq_rows = 1, tm
        kv_map = lambda i: (i // per, 0, i % per, 0)
    else:
        assert tm % t_len == 0
        n_seq, seq_rows = tm // t_len, t_len
        kv_map = lambda i: (i, 0, 0, 0)
    kv_spec = pl.BlockSpec((n_seq, n_heads, seq_rows, SB_HEAD_DIM), kv_map)
    kv_shape = jax.ShapeDtypeStruct((bsz, n_heads, t_len, SB_HEAD_DIM), F32)
    row = lambda w: pl.BlockSpec((tm, w), lambda i: (i, 0))
    kern = functools.partial(_inproj_kernel, sb_width=sb_width, seq_rows=seq_rows)
    return pl.pallas_call(
        kern, grid=(n // tm,),
        in_specs=[row(d), _const_spec(g.shape), _const_spec(w_in.shape)],
        out_specs=[row(3 * sb_width), kv_spec, kv_spec, row(width - 3 * sb_width)],
        out_shape=[jax.ShapeDtypeStruct((n, 3 * sb_width), BF16), kv_shape, kv_shape,
                   jax.ShapeDtypeStruct((n, width - 3 * sb_width), F32)],
        compiler_params=_params(("parallel",)), name="inproj",
    )(h, g, w_in)


ATTN_KEY_BLOCK = LANES
ATTN_WINDOW = 3 * ATTN_KEY_BLOCK


def _attn_kernel(q_ref, k_ref, v_ref, o_ref, *, tq, offset):
    tk = ATTN_KEY_BLOCK
    n_sub = ATTN_WINDOW // tk
    n_pairs = q_ref.shape[2] // LANES
    i = pl.program_id(1)
    q0 = offset + i * tq
    w_end = (offset + tq + tk - 1) // tk * tk + i * tq
    w_start = pl.multiple_of(jnp.maximum(w_end - ATTN_WINDOW, 0), tk)

    zero = jnp.zeros((), BF16)
    first_q = lax.broadcasted_iota(jnp.int32, (tq, LANES), 1) < SB_HEAD_DIM
    r_i = lax.broadcasted_iota(jnp.int32, (tk, tk), 0)
    c_i = lax.broadcasted_iota(jnp.int32, (tk, tk), 1)
    tri = jnp.where(r_i > c_i, 1.0, 0.0).astype(BF16)
    sum_mat = jnp.concatenate([tri, jnp.ones((tk, tk), BF16)], axis=1)
    sum_mat = jnp.concatenate([sum_mat, sum_mat], axis=0)

    def block_sums(log_keep):
        hi = log_keep.astype(BF16)
        lo = (log_keep - hi.astype(F32)).astype(BF16)
        sums = _dot(jnp.concatenate([hi, lo], axis=1), sum_mat)
        return sums[:, :tk], sums[:, tk:]

    def q_heads(pair):
        q2 = q_ref[0, :, pair * LANES:(pair + 1) * LANES]
        return jnp.where(first_q, q2, zero), jnp.where(first_q, zero, q2)

    def v_heads(vb):
        first_k = lax.broadcasted_iota(jnp.int32, vb.shape, 1) < SB_HEAD_DIM
        return jnp.concatenate([jnp.where(first_k, vb, zero), jnp.where(first_k, zero, vb)], axis=0)

    col_minus_row = (lax.broadcasted_iota(jnp.int32, (tq, ATTN_WINDOW), 1)
                     - lax.broadcasted_iota(jnp.int32, (tq, ATTN_WINDOW), 0))
    mask = col_minus_row < (q0 - w_start)
    n_heads = 2 * n_pairs
    scores = []
    for pair in range(n_pairs):
        kw = k_ref[0, pl.ds(w_start, ATTN_WINDOW), pair * LANES:(pair + 1) * LANES]
        scores += [_dot_nt(qh, kw) for qh in q_heads(pair)]
    log_betas, splits = [], []
    for z in scores:
        log_keep, log_beta = _log_sigmoids(z)
        log_keep = jnp.where(mask, log_keep, 0.0)
        hi = log_keep.astype(BF16)
        lo = (log_keep - hi.astype(F32)).astype(BF16)
        log_betas.append(log_beta)
        splits.append((hi, lo))
    sums = [[_dot(jnp.concatenate([hi[:, sub * tk:(sub + 1) * tk], lo[:, sub * tk:(sub + 1) * tk]], axis=1),
                  sum_mat) for sub in range(n_sub)] for hi, lo in splits]
    carries, probs = [], []
    for hd in range(n_heads):
        later, carry = [None] * n_sub, None
        for sub in reversed(range(n_sub)):
            local, total = sums[hd][sub][:, :tk], sums[hd][sub][:, tk:]
            later[sub] = local if carry is None else local + carry
            carry = total if carry is None else carry + total
        a = jnp.where(mask, jnp.exp(log_betas[hd] + jnp.concatenate(later, axis=1)), 0.0)
        probs.append(a.astype(BF16))
        carries.append(carry)
    accs = []
    for pair in range(n_pairs):
        vw = v_ref[0, pl.ds(w_start, ATTN_WINDOW), pair * LANES:(pair + 1) * LANES]
        accs.append(_dot(jnp.concatenate(probs[2 * pair:2 * pair + 2], axis=1), v_heads(vw)))

    def live(cs):
        m = cs[0]
        for c in cs[1:]:
            m = jnp.maximum(m, c)
        return (jnp.max(m) >= EXP_UNDERFLOW).astype(jnp.int32)

    def cond(state):
        return jnp.logical_and(state[0] >= 0, state[1] > 0)

    def body(state):
        j = state[0]
        cs, acs = list(state[2]), list(state[3])
        start = pl.multiple_of(j * tk, tk)
        for pair in range(n_pairs):
            cols = slice(pair * LANES, (pair + 1) * LANES)
            kb = k_ref[0, pl.ds(start, tk), cols]
            vb = v_ref[0, pl.ds(start, tk), cols]
            probs = []
            for hd, qh in enumerate(q_heads(pair)):
                z = _dot_nt(qh, kb)
                log_keep, log_beta = _log_sigmoids(z)
                local, total = block_sums(log_keep)
                carry = cs[2 * pair + hd]
                probs.append(jnp.exp(log_beta + local + carry).astype(BF16))
                cs[2 * pair + hd] = carry + total
            acs[pair] = acs[pair] + _dot(jnp.concatenate(probs, axis=1), v_heads(vb))
        return j - 1, live(cs), tuple(cs), tuple(acs)

    state = lax.while_loop(cond, body, (w_start // tk - 1, live(carries), tuple(carries), tuple(accs)))
    for pair in range(n_pairs):
        o_ref[0, :, pair * LANES:(pair + 1) * LANES] = state[3][pair].astype(o_ref.dtype)


def _attn_call(q_arr, q_blk, k_arr, k_blk, v_arr, v_blk, sb_width, offset, tq):
    bsz, t_q, _ = q_arr.shape
    t_k = k_arr.shape[1]
    assert t_q % tq == 0 and offset % tq == 0 and (tq == ATTN_KEY_BLOCK or tq == t_q)
    assert t_k % ATTN_KEY_BLOCK == 0 and t_k >= ATTN_WINDOW and offset + t_q <= t_k
    kern = functools.partial(_attn_kernel, tq=tq, offset=offset)
    kv_spec = lambda blk: pl.BlockSpec((1, t_k, sb_width), lambda b, i: (b, 0, blk),
                                       pipeline_mode=pl.Buffered(1))
    return pl.pallas_call(
        kern, grid=(bsz, t_q // tq),
        in_specs=[pl.BlockSpec((1, tq, sb_width), lambda b, i: (b, i, q_blk)), kv_spec(k_blk), kv_spec(v_blk)],
        out_specs=pl.BlockSpec((1, tq, sb_width), lambda b, i: (b, i, 0)),
        out_shape=jax.ShapeDtypeStruct((bsz, t_q, sb_width), BF16),
        compiler_params=_params(("parallel", "arbitrary")), name="stickbreak",
    )(q_arr, k_arr, v_arr)


def _rg_gates(c, xr_ref, cw_ref, cb_ref, wa_ref, wx_ref, hist_scr, tt):
    cols = slice(c * LANES, (c + 1) * LANES)
    x = xr_ref[:, cols]
    conv_w = cw_ref.shape[0]
    xcat = jnp.concatenate([hist_scr[:, cols], x], axis=0)
    xc = cb_ref[:, cols] + cw_ref[conv_w - 1:conv_w, cols] * x
    for k in range(1, conv_w):
        xc = xc + cw_ref[conv_w - 1 - k:conv_w - k, cols] * pltpu.roll(xcat, k, axis=0)[SUBLANES:]
    hist_scr[:, cols] = xcat[tt:]
    xcb = xc.astype(BF16)
    return xc, _dot(xcb, wa_ref[cols, cols]), _dot(xcb, wx_ref[cols, cols])


def _rg_scan(c, gates, xg_ref, ba_ref, bx_ref, lam_ref, hc_scr, rg_ref, tt):
    cols = slice(c * LANES, (c + 1) * LANES)
    xc, r_pre, ig_pre = gates
    r = _sigmoid(r_pre + ba_ref[:, cols])
    ig = _sigmoid(ig_pre + bx_ref[:, cols])
    log_a = (-RG_C * _softplus(-lam_ref[:, cols])) * r
    a = jnp.exp(log_a)
    b = jnp.sqrt(_neg_expm1(2.0 * log_a)) * (ig * xc)

    row = lax.broadcasted_iota(jnp.int32, a.shape, 0)
    s = 1
    while s < tt:
        valid = row >= s
        b = b + a * jnp.where(valid, pltpu.roll(b, s, axis=0), 0.0)
        a = a * jnp.where(valid, pltpu.roll(a, s, axis=0), 1.0)
        s *= 2
    h = b + a * hc_scr[SUBLANES - 1:SUBLANES, cols]
    hc_scr[:, cols] = h[tt - SUBLANES:]
    rg_ref[:, cols] = (h * _gelu_tanh(xg_ref[:, cols])).astype(rg_ref.dtype)


def _rg_kernel(xr_ref, xg_ref, cst_ref, h0_ref, *refs, tt):
    consts, (rg_ref, cnew_ref, hlast_ref, hist_scr, hc_scr) = refs[:7], refs[7:]
    ti = pl.program_id(1)

    @pl.when(ti == 0)
    def _():
        hist_scr[...] = cst_ref[0]
        hc_scr[...] = h0_ref[0]

    cw_ref, cb_ref, wa_ref, ba_ref, wx_ref, bx_ref, lam_ref = consts
    for c in range(hist_scr.shape[1] // LANES):
        gates = _rg_gates(c, xr_ref.at[0], cw_ref, cb_ref, wa_ref, wx_ref, hist_scr, tt)
        _rg_scan(c, gates, xg_ref.at[0], ba_ref, bx_ref, lam_ref, hc_scr, rg_ref.at[0], tt)

    @pl.when(ti == pl.num_programs(1) - 1)
    def _():
        cnew_ref[0] = hist_scr[...]
        hlast_ref[0] = hc_scr[...]


def _rg_call(xrg, conv_state8, h0_8, cw, cb, wa, ba, wx, bx, lam, tile=256):
    bsz, t_len, two_w = xrg.shape
    w = two_w // 2
    tt = _row_tile(t_len, tile)
    kern = functools.partial(_rg_kernel, tt=tt)
    state_spec = pl.BlockSpec((1, SUBLANES, w), lambda b, t: (b, 0, 0))
    state_shape = jax.ShapeDtypeStruct((bsz, SUBLANES, w), F32)
    return pl.pallas_call(
        kern, grid=(bsz, t_len // tt),
        in_specs=[pl.BlockSpec((1, tt, w), lambda b, t: (b, t, 0)),
                  pl.BlockSpec((1, tt, w), lambda b, t: (b, t, 1)),
                  state_spec, state_spec,
                  _const_spec(cw.shape), _const_spec(cb.shape), _const_spec(wa.shape), _const_spec(ba.shape),
                  _const_spec(wx.shape), _const_spec(bx.shape), _const_spec(lam.shape)],
        out_specs=[pl.BlockSpec((1, tt, w), lambda b, t: (b, t, 0)), state_spec, state_spec],
        out_shape=[jax.ShapeDtypeStruct((bsz, t_len, w), BF16), state_shape, state_shape],
        scratch_shapes=[pltpu.VMEM((SUBLANES, w), F32), pltpu.VMEM((SUBLANES, w), F32)],
        compiler_params=_params(("parallel", "arbitrary")), name="rglru",
    )(xrg, xrg, conv_state8, h0_8, cw, cb, wa, ba, wx, bx, lam)


def _rgffn_kernel(xr_ref, xg_ref, cst_ref, h0_ref, *refs, tt, tiles_per_seq, has_final, ff_splits):
    rg_consts, refs = refs[:7], refs[7:]
    h_ref, sb_ref, wo_ref, g_ref, wg_ref, wu_ref, wd_ref = refs[:7]
    ple = refs[7:11]
    gf_ref = refs[11] if has_final else None
    o_ref, cnew_ref, hlast_ref, hist_scr, hc_scr, rg_scr = refs[11 + has_final:]
    s = pl.program_id(0)
    n_tiles = pl.num_programs(0) - 1
    ti = lax.rem(jnp.minimum(s, n_tiles - 1), tiles_per_seq)

    @pl.when(s == 0)
    def _():
        rg_scr[...] = jnp.zeros_like(rg_scr)

    @pl.when(ti == 0)
    def _():
        hist_scr[...] = cst_ref[0]
        hc_scr[...] = h0_ref[0]

    rg_prev = rg_scr[...]
    half = sb_ref.shape[-1]
    h = h_ref[...] + _dot(sb_ref[...], wo_ref[:half, :]) + _dot(rg_prev, wo_ref[half:, :])
    cw_ref, cb_ref, wa_ref, ba_ref, wx_ref, bx_ref, lam_ref = rg_consts
    rg_chunks = [functools.partial(_rg_scan, c, _rg_gates(c, xr_ref, cw_ref, cb_ref, wa_ref, wx_ref, hist_scr, tt),
                                   xg_ref, ba_ref, bx_ref, lam_ref, hc_scr, rg_scr, tt)
                 for c in range(rg_scr.shape[1] // LANES)]
    o_ref[...] = _ffn_tile(h, g_ref, wg_ref, wu_ref, wd_ref, ple, gf_ref, ff_splits, side_work=rg_chunks)

    @pl.when(jnp.logical_and(ti == tiles_per_seq - 1, s < n_tiles))
    def _():
        cnew_ref[0] = hist_scr[...]
        hlast_ref[0] = hc_scr[...]


def _rgffn_call(xrg, t_len, conv_state8, h0_8, rg_consts, h, sb, wo, g, wg, wu, wd, ple, final_g, tile):
    n, d = h.shape
    w = xrg.shape[1] // 2
    tt = tile
    assert t_len % tt == 0
    tiles_per_seq, n_tiles = t_len // tt, n // tt
    p, gp, wpg, wpp = ple
    rg_tile = lambda s: jnp.minimum(s, n_tiles - 1)
    ffn_tile = lambda s: jnp.maximum(s - 1, 0)
    state_spec = pl.BlockSpec((1, SUBLANES, w), lambda s: (rg_tile(s) // tiles_per_seq, 0, 0))
    state_shape = jax.ShapeDtypeStruct((conv_state8.shape[0], SUBLANES, w), F32)
    ffn_row = lambda width: pl.BlockSpec((tt, width), lambda s: (ffn_tile(s), 0))
    mid = [wo, g, wg, wu, wd]
    tail = [gp, wpg, wpp] + ([final_g] if final_g is not None else [])
    kern = functools.partial(_rgffn_kernel, tt=tt, tiles_per_seq=tiles_per_seq,
                             has_final=final_g is not None, ff_splits=_ff_splits(wg.shape[1], 4))
    return pl.pallas_call(
        kern, grid=(n_tiles + 1,),
        in_specs=[pl.BlockSpec((tt, w), lambda s: (rg_tile(s), 0)), pl.BlockSpec((tt, w), lambda s: (rg_tile(s), 1)),
                  state_spec, state_spec]
                 + [_const_spec(c.shape) for c in rg_consts]
                 + [ffn_row(d), ffn_row(sb.shape[1])] + [_const_spec(c.shape) for c in mid]
                 + [ffn_row(p.shape[1])] + [_const_spec(c.shape) for c in tail],
        out_specs=[ffn_row(d), state_spec, state_spec],
        out_shape=[jax.ShapeDtypeStruct((n, d), F32), state_shape, state_shape],
        scratch_shapes=[pltpu.VMEM((SUBLANES, w), F32), pltpu.VMEM((SUBLANES, w), F32), pltpu.VMEM((tt, w), BF16)],
        compiler_params=_params(("arbitrary",)), name="rgffn",
    )(xrg, xrg, conv_state8, h0_8, *rg_consts, h, sb, *mid, p, *tail)


def _convmod_glu(x, g_ref, w1_ref, b1_ref, ucat_scr, hist):
    d = x.shape[1]
    hn = _rms(x, g_ref[...]).astype(BF16)
    ag = _dot(hn, w1_ref[...]) + b1_ref[...]
    ucat_scr[hist:, :] = ag[:, :d] * _sigmoid(ag[:, d:])


def _conv_chunk(c, dw_ref, dwb_ref, ucat_scr, conv_scr, tt, hist, taps):
    cols = slice(c * LANES, (c + 1) * LANES)
    w_rows = [dw_ref[k:k + 1, cols] for k in range(taps)]
    bias = dwb_ref[:, cols]
    for r0 in range(0, tt, CONV_ROW_CHUNK):
        win = ucat_scr[r0:r0 + hist + CONV_ROW_CHUNK, cols]
        acc = bias + w_rows[taps - 1] * win[hist:]
        for r in range(SUBLANES):
            base = win if r == 0 else pltpu.roll(win, r, axis=0)
            for m in range((taps - 1 - r) // SUBLANES + 1):
                k = SUBLANES * m + r
                if k > 0:
                    lo = hist - SUBLANES * m
                    acc = acc + w_rows[taps - 1 - k] * base[lo:lo + CONV_ROW_CHUNK]
        conv_scr[r0:r0 + CONV_ROW_CHUNK, cols] = acc


def _convmod_out(x, lng_ref, lnb_ref, w2_ref, b2_ref, ucat_scr, conv_scr, tt, hist):
    new_hist = ucat_scr[tt:, :]
    ucat_scr[:hist, :] = new_hist
    cv = conv_scr[...]
    mu = jnp.mean(cv, axis=-1, keepdims=True)
    xc = cv - mu
    ln = xc * lax.rsqrt(jnp.mean(xc * xc, axis=-1, keepdims=True) + NORM_EPS) * lng_ref[...] + lnb_ref[...]
    act = (ln * _sigmoid(ln)).astype(BF16)
    return x + _dot(act, w2_ref[...]) + b2_ref[...], new_hist


def _convmod_tile(x, g_ref, w1_ref, b1_ref, dw_ref, dwb_ref, lng_ref, lnb_ref, w2_ref, b2_ref,
                  ucat_scr, conv_scr, *, tt, hist, taps):
    _convmod_glu(x, g_ref, w1_ref, b1_ref, ucat_scr, hist)
    for c in range(x.shape[1] // LANES):
        _conv_chunk(c, dw_ref, dwb_ref, ucat_scr, conv_scr, tt, hist, taps)
    return _convmod_out(x, lng_ref, lnb_ref, w2_ref, b2_ref, ucat_scr, conv_scr, tt, hist)


def _convmod_kernel(h_ref, cst_ref, *refs, tt, hist, taps):
    consts, (o_ref, cnew_ref, ucat_scr, conv_scr) = refs[:9], refs[9:]
    ti = pl.program_id(1)

    @pl.when(ti == 0)
    def _():
        ucat_scr[:hist, :] = cst_ref[0]

    o_ref[0], new_hist = _convmod_tile(h_ref[0], *consts, ucat_scr, conv_scr, tt=tt, hist=hist, taps=taps)

    @pl.when(ti == pl.num_programs(1) - 1)
    def _():
        cnew_ref[0] = new_hist


def _convmod_call(h, conv_state_pad, g, w1, b1, dw, dwb, lng, lnb, w2, b2, tile=256):
    bsz, t_len, d = h.shape
    hist = conv_state_pad.shape[1]
    taps = dw.shape[0]
    tt = _row_tile(t_len, tile)
    assert tt >= hist and hist % SUBLANES == 0 and taps - 1 <= hist
    kern = functools.partial(_convmod_kernel, tt=tt, hist=hist, taps=taps)
    state_spec = pl.BlockSpec((1, hist, d), lambda b, t: (b, 0, 0))
    consts = [g, w1, b1, dw, dwb, lng, lnb, w2, b2]
    return pl.pallas_call(
        kern, grid=(bsz, t_len // tt),
        in_specs=[pl.BlockSpec((1, tt, d), lambda b, t: (b, t, 0)), state_spec]
                 + [_const_spec(c.shape) for c in consts],
        out_specs=[pl.BlockSpec((1, tt, d), lambda b, t: (b, t, 0)), state_spec],
        out_shape=[jax.ShapeDtypeStruct((bsz, t_len, d), F32),
                   jax.ShapeDtypeStruct((bsz, hist, d), F32)],
        scratch_shapes=[pltpu.VMEM((hist + tt, d), F32), pltpu.VMEM((tt, d), F32)],
        compiler_params=_params(("parallel", "arbitrary")), name="convmod",
    )(h, conv_state_pad, *consts)


def _convffn_kernel(h_ref, cst_ref, *refs, tt, hist, taps, tiles_per_seq, has_final, ff_splits):
    cm_consts, refs = refs[:9], refs[9:]
    g_ref, wg_ref, wu_ref, wd_ref = refs[:4]
    ple = refs[4:8]
    gf_ref = refs[8] if has_final else None
    o_ref, cnew_ref, ucat_scr, conv_scr, h_scr = refs[8 + has_final:]
    s = pl.program_id(0)
    n_tiles = pl.num_programs(0) - 1
    ti = lax.rem(jnp.minimum(s, n_tiles - 1), tiles_per_seq)

    @pl.when(s == 0)
    def _():
        h_scr[...] = jnp.zeros_like(h_scr)

    @pl.when(ti == 0)
    def _():
        ucat_scr[:hist, :] = cst_ref[0]

    g_cm, w1_ref, b1_ref, dw_ref, dwb_ref, lng_ref, lnb_ref, w2_ref, b2_ref = cm_consts
    h_prev = h_scr[...]
    x = h_ref[...]
    _convmod_glu(x, g_cm, w1_ref, b1_ref, ucat_scr, hist)
    conv_chunks = [functools.partial(_conv_chunk, c, dw_ref, dwb_ref, ucat_scr, conv_scr, tt, hist, taps)
                   for c in range(x.shape[1] // LANES)]
    o_ref[...] = _ffn_tile(h_prev, g_ref, wg_ref, wu_ref, wd_ref, ple, gf_ref, ff_splits, side_work=conv_chunks)
    h_new, new_hist = _convmod_out(x, lng_ref, lnb_ref, w2_ref, b2_ref, ucat_scr, conv_scr, tt, hist)
    h_scr[...] = h_new

    @pl.when(jnp.logical_and(ti == tiles_per_seq - 1, s < n_tiles))
    def _():
        cnew_ref[0] = new_hist


def _convffn_call(h, t_len, conv_state_pad, cm_consts, g, wg, wu, wd, ple, final_g, tile):
    n, d = h.shape
    hist = conv_state_pad.shape[1]
    taps = cm_consts[3].shape[0]
    tt = tile
    assert t_len % tt == 0 and tt >= hist and hist % SUBLANES == 0 and taps - 1 <= hist
    tiles_per_seq, n_tiles = t_len // tt, n // tt
    p, gp, wpg, wpp = ple
    conv_tile = lambda s: jnp.minimum(s, n_tiles - 1)
    ffn_tile = lambda s: jnp.maximum(s - 1, 0)
    state_spec = pl.BlockSpec((1, hist, d), lambda s: (conv_tile(s) // tiles_per_seq, 0, 0))
    consts = list(cm_consts) + [g, wg, wu, wd]
    tail = [gp, wpg, wpp] + ([final_g] if final_g is not None else [])
    kern = functools.partial(_convffn_kernel, tt=tt, hist=hist, taps=taps, tiles_per_seq=tiles_per_seq,
                             has_final=final_g is not None, ff_splits=_ff_splits(wg.shape[1], 4))
    return pl.pallas_call(
        kern, grid=(n_tiles + 1,),
        in_specs=[pl.BlockSpec((tt, d), lambda s: (conv_tile(s), 0)), state_spec]
                 + [_const_spec(c.shape) for c in consts]
                 + [pl.BlockSpec((tt, p.shape[1]), lambda s: (ffn_tile(s), 0))]
                 + [_const_spec(c.shape) for c in tail],
        out_specs=[pl.BlockSpec((tt, d), lambda s: (ffn_tile(s), 0)), state_spec],
        out_shape=[jax.ShapeDtypeStruct((n, d), F32),
                   jax.ShapeDtypeStruct((conv_state_pad.shape[0], hist, d), F32)],
        scratch_shapes=[pltpu.VMEM((hist + tt, d), F32), pltpu.VMEM((tt, d), F32), pltpu.VMEM((tt, d), F32)],
        compiler_params=_params(("arbitrary",)), name="convffn",
    )(h, conv_state_pad, *consts, p, *tail)


def _block_diag(w):
    g, i, j = w.shape
    eye = jnp.eye(g, dtype=w.dtype)
    return (eye[:, None, :, None] * w[:, :, None, :]).reshape(g * i, g * j)


def _pad_rows_front(x, rows):
    return jnp.pad(x, ((0, 0), (rows - x.shape[1], 0), (0, 0)))


def _trunk(x, p, w, sb_k, sb_v, rg_conv, rg_h, cm_conv):
    bsz, t_len, d = x.shape
    n = bsz * t_len
    fresh = sb_k is None
    depth = w['norm_ffn'].shape[0]
    row2 = lambda v: v.reshape(1, -1)
    bf = lambda v: v.astype(BF16)
    h = x.reshape(n, d)
    out_k, out_v, out_rc, out_rh, out_cc = [], [], [], [], []
    for i in range(depth):
        j = i // 2
        ffn = lambda hh, s, **kw: _ffn_call(
            hh, row2(w['norm_ffn'][i, s]), bf(w['w_ffn_gate'][i, s]), bf(w['w_ffn_up'][i, s]),
            bf(w['w_ffn_down'][i, s]), **kw)
        ple = (p[i].reshape(n, -1), row2(w['norm_ple'][i]), bf(w['w_ple_gate'][i]), bf(w['w_ple_proj'][i]))
        final_g = row2(w['norm_final']) if i == depth - 1 else None
        h = ffn(h, 0)
        if i % 2 == 0:
            rg_w = w['rg_lambda'].shape[1]
            sb_w = (w['w_ab_in'].shape[2] - 2 * rg_w) // 3
            qkv, k_new, v_new, xrg = _inproj_call(h, row2(w['norm_mix'][i]), bf(w['w_ab_in'][j]),
                                                  bsz, t_len, sb_w)
            qkv3 = qkv.reshape(bsz, t_len, 3 * sb_w)
            if fresh:
                sb = _attn_call(qkv3, 0, qkv3, 1, qkv3, 2, sb_w, 0, ATTN_KEY_BLOCK)
                conv0 = jnp.zeros((bsz, SUBLANES, rg_w), F32)
                h0 = jnp.zeros((bsz, SUBLANES, rg_w), F32)
            else:
                past = sb_k.shape[3]
                t_k = -(-(past + t_len) // ATTN_KEY_BLOCK) * ATTN_KEY_BLOCK

                def with_cache(cache, blk):
                    tm_major = cache.transpose(0, 2, 1, 3).reshape(bsz, past, sb_w).astype(BF16)
                    new = qkv3[:, :, blk * sb_w:(blk + 1) * sb_w]
                    full = jnp.concatenate([tm_major, new], axis=1)
                    return jnp.pad(full, ((0, 0), (0, t_k - past - t_len), (0, 0)))

                sb = _attn_call(qkv3, 0, with_cache(sb_k[j], 1), 0, with_cache(sb_v[j], 2), 0,
                                sb_w, past, t_len)
                conv0 = _pad_rows_front(rg_conv[j], SUBLANES)
                h0 = jnp.broadcast_to(rg_h[j][:, None, :], (bsz, SUBLANES, rg_w))
            assert LANES % w['rg_gate_a_w'].shape[2] == 0
            rg_consts = (w['rg_conv_w'][j], row2(w['rg_conv_b'][j]),
                         bf(_block_diag(w['rg_gate_a_w'][j])), row2(w['rg_gate_a_b'][j]),
                         bf(_block_diag(w['rg_gate_x_w'][j])), row2(w['rg_gate_x_b'][j]), row2(w['rg_lambda'][j]))
            if t_len >= FUSED_TILE:
                h, rc8, rh8 = _rgffn_call(
                    xrg, t_len, conv0, h0, rg_consts, h, sb.reshape(n, sb_w), bf(w['w_ab_out'][j]),
                    row2(w['norm_ffn'][i, 1]), bf(w['w_ffn_gate'][i, 1]), bf(w['w_ffn_up'][i, 1]),
                    bf(w['w_ffn_down'][i, 1]), ple, final_g, FUSED_TILE)
            else:
                rg, rc8, rh8 = _rg_call(xrg.reshape(bsz, t_len, 2 * rg_w), conv0, h0, *rg_consts)
                mix = (sb.reshape(n, sb_w), rg.reshape(n, rg_w), bf(w['w_ab_out'][j]))
                h = ffn(h, 1, mix=mix, ple=ple, final_g=final_g)
            out_k.append(k_new)
            out_v.append(v_new)
            out_rc.append(rc8[:, SUBLANES - (w['rg_conv_w'].shape[1] - 1):])
            out_rh.append(rh8[:, SUBLANES - 1])
        else:
            taps = w['cm_dw_w'].shape[1]
            hist = -(-(taps - 1) // SUBLANES) * SUBLANES
            cc0 = jnp.zeros((bsz, hist, d), F32) if fresh else _pad_rows_front(cm_conv[j], hist)
            cm_consts = (row2(w['norm_mix'][i]), bf(w['cm_pw1_w'][j]), row2(w['cm_pw1_b'][j]), w['cm_dw_w'][j],
                         row2(w['cm_dw_b'][j]), row2(w['cm_ln_g'][j]), row2(w['cm_ln_b'][j]),
                         bf(w['cm_pw2_w'][j]), row2(w['cm_pw2_b'][j]))
            if t_len >= FUSED_TILE:
                h, cc_new = _convffn_call(
                    h, t_len, cc0, cm_consts, row2(w['norm_ffn'][i, 1]), bf(w['w_ffn_gate'][i, 1]),
                    bf(w['w_ffn_up'][i, 1]), bf(w['w_ffn_down'][i, 1]), ple, final_g, FUSED_TILE)
            else:
                h3, cc_new = _convmod_call(h.reshape(bsz, t_len, d), cc0, *cm_consts)
                h = ffn(h3.reshape(n, d), 1, ple=ple, final_g=final_g)
            out_cc.append(cc_new[:, hist - (taps - 1):])
    y = h.reshape(bsz, t_len, d)
    return y, jnp.stack(out_k), jnp.stack(out_v), jnp.stack(out_rc), jnp.stack(out_rh), jnp.stack(out_cc)


def kernel(x_prompt, x_sample, cache_sb_k, cache_sb_v, state_rg_conv, state_rg_h, state_cm_conv,
           p_prompt, p_sample, norm_ffn, w_ffn_gate, w_ffn_up, w_ffn_down, norm_mix, w_ab_in, w_ab_out,
           rg_conv_w, rg_conv_b, rg_gate_a_w, rg_gate_a_b, rg_gate_x_w, rg_gate_x_b, rg_lambda,
           cm_pw1_w, cm_pw1_b, cm_dw_w, cm_dw_b, cm_ln_g, cm_ln_b, cm_pw2_w, cm_pw2_b,
           norm_ple, w_ple_gate, w_ple_proj, norm_final):
    w = dict(norm_ffn=norm_ffn, w_ffn_gate=w_ffn_gate, w_ffn_up=w_ffn_up, w_ffn_down=w_ffn_down,
             norm_mix=norm_mix, w_ab_in=w_ab_in, w_ab_out=w_ab_out, rg_conv_w=rg_conv_w, rg_conv_b=rg_conv_b,
             rg_gate_a_w=rg_gate_a_w, rg_gate_a_b=rg_gate_a_b, rg_gate_x_w=rg_gate_x_w, rg_gate_x_b=rg_gate_x_b,
             rg_lambda=rg_lambda, cm_pw1_w=cm_pw1_w, cm_pw1_b=cm_pw1_b, cm_dw_w=cm_dw_w, cm_dw_b=cm_dw_b,
             cm_ln_g=cm_ln_g, cm_ln_b=cm_ln_b, cm_pw2_w=cm_pw2_w, cm_pw2_b=cm_pw2_b,
             norm_ple=norm_ple, w_ple_gate=w_ple_gate, w_ple_proj=w_ple_proj, norm_final=norm_final)
    y_p, k_p, v_p, rc_p, rh_p, cc_p = _trunk(x_prompt, p_prompt, w, None, None, None, None, None)
    y_s, k_s, v_s, rc_s, rh_s, cc_s = _trunk(x_sample, p_sample, w, cache_sb_k, cache_sb_v,
                                             state_rg_conv, state_rg_h, state_cm_conv)
    return (y_p, y_s, k_p, v_p, rc_p, rh_p, cc_p, k_s, v_s, rc_s, rh_s, cc_s)
```

```python
import functools
import math

import jax
import jax.numpy as jnp
from jax import lax
from jax.experimental import pallas as pl
from jax.experimental.pallas import tpu as pltpu

NORM_EPS = 1e-6
SB_HEAD_DIM = 64
RG_C = 8.0
LANES = 128
SUBLANES = 8
MXU_DIM = 256
VMEM_LIMIT_BYTES = 56 * 1024 * 1024
EXP_UNDERFLOW = -105.0
FUSED_TILE = 512
CONV_ROW_CHUNK = 64

BF16 = jnp.bfloat16
F32 = jnp.float32


def _dot(a, b):
    return jnp.dot(a, b, preferred_element_type=F32)


def _dot_nt(a, b):
    return lax.dot_general(a, b, (((1,), (1,)), ((), ())), preferred_element_type=F32)


def _sigmoid(x):
    return 1.0 / (1.0 + jnp.exp(-x))


def _softplus(x):
    return jnp.maximum(x, 0.0) + jnp.log(1.0 + jnp.exp(-jnp.abs(x)))


def _log_sigmoids(z):
    nz = -z
    log_keep = jnp.minimum(nz, 0.0) - jnp.log(1.0 + jnp.exp(jnp.minimum(z, nz)))
    return log_keep, z + log_keep


def _neg_expm1(t):
    series = 1.0 / 24.0
    for c in (1.0 / 6.0, 0.5, 1.0):
        series = series * t + c
    return jnp.where(t > -0.03, -t * series, 1.0 - jnp.exp(t))


def _rms(x, g):
    return x * lax.rsqrt(jnp.mean(x * x, axis=-1, keepdims=True) + NORM_EPS) * g


def _gelu_tanh(x):
    return 0.5 * x * (1.0 + jnp.tanh(math.sqrt(2.0 / math.pi) * (x + 0.044715 * (x * x * x))))


def _const_spec(shape):
    zeros = (0,) * len(shape)
    return pl.BlockSpec(shape, lambda *_: zeros, pipeline_mode=pl.Buffered(1))


def _params(semantics):
    return pltpu.CompilerParams(dimension_semantics=semantics, vmem_limit_bytes=VMEM_LIMIT_BYTES)


def _row_tile(n, target):
    t = min(n, target)
    assert n % t == 0, (n, t)
    return t


def _ffn_tile(h, g_ref, wg_ref, wu_ref, wd_ref, ple, gf_ref, ff_splits, side_work=()):
    xn = _rms(h, g_ref[...]).astype(BF16)
    acc = None
    side_work = list(side_work)
    per_split = -(-len(side_work) // len(ff_splits))
    for s, e in ff_splits:
        gate = _dot(xn, wg_ref[:, s:e])
        up = _dot(xn, wu_ref[:, s:e])
        for work in side_work[:per_split]:
            work()
        side_work = side_work[per_split:]
        act = (gate * _sigmoid(gate) * up).astype(BF16)
        part = _dot(act, wd_ref[s:e, :])
        acc = part if acc is None else acc + part
    h = h + 0.5 * acc
    if ple is not None:
        p_ref, gp_ref, wpg_ref, wpp_ref = ple
        gate = _sigmoid(_dot(_rms(h, gp_ref[...]).astype(BF16), wpg_ref[...]))
        h = h + gate * _dot(p_ref[...].astype(BF16), wpp_ref[...])
    if gf_ref is not None:
        h = _rms(h, gf_ref[...])
    return h


def _ffn_kernel(*refs, has_mix, has_ple, has_final, ff_splits):
    it = iter(refs)
    h_ref = next(it)
    if has_mix:
        sb_ref, rg_ref, wo_ref = next(it), next(it), next(it)
    g_ref, wg_ref, wu_ref, wd_ref = next(it), next(it), next(it), next(it)
    ple = (next(it), next(it), next(it), next(it)) if has_ple else None
    gf_ref = next(it) if has_final else None
    o_ref = next(it)

    h = h_ref[...]
    if has_mix:
        half = sb_ref.shape[-1]
        h = h + _dot(sb_ref[...], wo_ref[:half, :]) + _dot(rg_ref[...], wo_ref[half:, :])
    o_ref[...] = _ffn_tile(h, g_ref, wg_ref, wu_ref, wd_ref, ple, gf_ref, ff_splits)


def _ff_splits(d_ff, n=2):
    if d_ff % MXU_DIM == 0 and d_ff // MXU_DIM >= n:
        tiles = d_ff // MXU_DIM
        edges = [(-(-tiles * k // n)) * MXU_DIM for k in range(n + 1)]
        return tuple(zip(edges[:-1], edges[1:]))
    return ((0, d_ff),)


def _ffn_call(h, g, wg, wu, wd, mix=None, ple=None, final_g=None, tile=512):
    n, d = h.shape
    d_ff = wg.shape[1]
    tm = _row_tile(n, tile)
    row = lambda w: pl.BlockSpec((tm, w), lambda i: (i, 0))
    args, specs = [h], [row(d)]
    if mix is not None:
        sb, rg, wo = mix
        args += [sb, rg, wo]
        specs += [row(sb.shape[1]), row(rg.shape[1]), _const_spec(wo.shape)]
    args += [g, wg, wu, wd]
    specs += [_const_spec(g.shape), _const_spec(wg.shape), _const_spec(wu.shape), _const_spec(wd.shape)]
    if ple is not None:
        p, gp, wpg, wpp = ple
        args += [p, gp, wpg, wpp]
        specs += [row(p.shape[1]), _const_spec(gp.shape), _const_spec(wpg.shape), _const_spec(wpp.shape)]
    if final_g is not None:
        args.append(final_g)
        specs.append(_const_spec(final_g.shape))
    kern = functools.partial(_ffn_kernel, has_mix=mix is not None, has_ple=ple is not None,
                             has_final=final_g is not None, ff_splits=_ff_splits(d_ff))
    return pl.pallas_call(
        kern, grid=(n // tm,), in_specs=specs, out_specs=row(d),
        out_shape=jax.ShapeDtypeStruct((n, d), F32),
        compiler_params=_params(("parallel",)), name="ffn",
    )(*args)


def _inproj_kernel(h_ref, g_ref, w_ref, qkv_ref, k_ref, v_ref, xrg_ref, *, sb_width, seq_rows):
    hn = _rms(h_ref[...], g_ref[...]).astype(BF16)
    u = _dot(hn, w_ref[...])
    scale = SB_HEAD_DIM ** -0.5
    qkv_ref[:, :sb_width] = (u[:, :sb_width] * scale).astype(BF16)
    qkv_ref[:, sb_width:] = u[:, sb_width:3 * sb_width].astype(BF16)
    xrg_ref[...] = u[:, 3 * sb_width:]
    n_seq, n_heads = k_ref.shape[0], k_ref.shape[1]
    for s in range(n_seq):
        rows = slice(s * seq_rows, (s + 1) * seq_rows)
        for hd in range(n_heads):
            c0 = sb_width + hd * SB_HEAD_DIM
            k_ref[s, hd] = u[rows, c0:c0 + SB_HEAD_DIM]
            v_ref[s, hd] = u[rows, c0 + sb_width:c0 + sb_width + SB_HEAD_DIM]


def _inproj_call(h, g, w_in, bsz, t_len, sb_width, tile=512):
    n, d = h.shape
    width = w_in.shape[1]
    n_heads = sb_width // SB_HEAD_DIM
    tm = _row_tile(n, tile)
    if tm <= t_len:
        assert t_len % tm == 0
        per = t_len // tm
        n_seq, seq_rows = 1, tm
        kv_map = lambda i: (i // per, 0, i % per, 0)
    else:
        assert tm % t_len == 0
        n_seq, seq_rows = tm // t_len, t_len
        kv_map = lambda i: (i, 0, 0, 0)
    kv_spec = pl.BlockSpec((n_seq, n_heads, seq_rows, SB_HEAD_DIM), kv_map)
    kv_shape = jax.ShapeDtypeStruct((bsz, n_heads, t_len, SB_HEAD_DIM), F32)
    row = lambda w: pl.BlockSpec((tm, w), lambda i: (i, 0))
    kern = functools.partial(_inproj_kernel, sb_width=sb_width, seq_rows=seq_rows)
    return pl.pallas_call(
        kern, grid=(n // tm,),
        in_specs=[row(d), _const_spec(g.shape), _const_spec(w_in.shape)],
        out_specs=[row(3 * sb_width), kv_spec, kv_spec, row(width - 3 * sb_width)],
        out_shape=[jax.ShapeDtypeStruct((n, 3 * sb_width), BF16), kv_shape, kv_shape,
                   jax.ShapeDtypeStruct((n, width - 3 * sb_width), F32)],
        compiler_params=_params(("parallel",)), name="inproj",
    )(h, g, w_in)


ATTN_KEY_BLOCK = LANES
ATTN_WINDOW = 3 * ATTN_KEY_BLOCK


def _attn_kernel(q_ref, k_ref, v_ref, o_ref, *, tq, offset):
    tk = ATTN_KEY_BLOCK
    n_sub = ATTN_WINDOW // tk
    n_pairs = q_ref.shape[2] // LANES
    i = pl.program_id(1)
    q0 = offset + i * tq
    w_end = (offset + tq + tk - 1) // tk * tk + i * tq
    w_start = pl.multiple_of(jnp.maximum(w_end - ATTN_WINDOW, 0), tk)

    zero = jnp.zeros((), BF16)
    first_q = lax.broadcasted_iota(jnp.int32, (tq, LANES), 1) < SB_HEAD_DIM
    r_i = lax.broadcasted_iota(jnp.int32, (tk, tk), 0)
    c_i = lax.broadcasted_iota(jnp.int32, (tk, tk), 1)
    tri = jnp.where(r_i > c_i, 1.0, 0.0).astype(BF16)
    sum_mat = jnp.concatenate([tri, jnp.ones((tk, tk), BF16)], axis=1)
    sum_mat = jnp.concatenate([sum_mat, sum_mat], axis=0)

    def block_sums(log_keep):
        hi = log_keep.astype(BF16)
        lo = (log_keep - hi.astype(F32)).astype(BF16)
        sums = _dot(jnp.concatenate([hi, lo], axis=1), sum_mat)
        return sums[:, :tk], sums[:, tk:]

    def q_heads(pair):
        q2 = q_ref[0, :, pair * LANES:(pair + 1) * LANES]
        return jnp.where(first_q, q2, zero), jnp.where(first_q, zero, q2)

    def v_heads(vb):
        first_k = lax.broadcasted_iota(jnp.int32, vb.shape, 1) < SB_HEAD_DIM
        return jnp.concatenate([jnp.where(first_k, vb, zero), jnp.where(first_k, zero, vb)], axis=0)

    col_minus_row = (lax.broadcasted_iota(jnp.int32, (tq, ATTN_WINDOW), 1)
                     - lax.broadcasted_iota(jnp.int32, (tq, ATTN_WINDOW), 0))
    mask = col_minus_row < (q0 - w_start)
    n_heads = 2 * n_pairs
    scores = []
    for pair in range(n_pairs):
        kw = k_ref[0, pl.ds(w_start, ATTN_WINDOW), pair * LANES:(pair + 1) * LANES]
        scores += [_dot_nt(qh, kw) for qh in q_heads(pair)]
    log_betas, splits = [], []
    for z in scores:
        log_keep, log_beta = _log_sigmoids(z)
        log_keep = jnp.where(mask, log_keep, 0.0)
        hi = log_keep.astype(BF16)
        lo = (log_keep - hi.astype(F32)).astype(BF16)
        log_betas.append(log_beta)
        splits.append((hi, lo))
    sums = [[_dot(jnp.concatenate([hi[:, sub * tk:(sub + 1) * tk], lo[:, sub * tk:(sub + 1) * tk]], axis=1),
                  sum_mat) for sub in range(n_sub)] for hi, lo in splits]
    carries, probs = [], []
    for hd in range(n_heads):
        later, carry = [None] * n_sub, None
        for sub in reversed(range(n_sub)):
            local, total = sums[hd][sub][:, :tk], sums[hd][sub][:, tk:]
            later[sub] = local if carry is None else local + carry
            carry = total if carry is None else carry + total
        a = jnp.where(mask, jnp.exp(log_betas[hd] + jnp.concatenate(later, axis=1)), 0.0)
        probs.append(a.astype(BF16))
        carries.append(carry)
    accs = []
    for pair in range(n_pairs):
        vw = v_ref[0, pl.ds(w_start, ATTN_WINDOW), pair * LANES:(pair + 1) * LANES]
        accs.append(_dot(jnp.concatenate(probs[2 * pair:2 * pair + 2], axis=1), v_heads(vw)))

    def live(cs):
        m = cs[0]
        for c in cs[1:]:
            m = jnp.maximum(m, c)
        return (jnp.max(m) >= EXP_UNDERFLOW).astype(jnp.int32)

    def cond(state):
        return jnp.logical_and(state[0] >= 0, state[1] > 0)

    def body(state):
        j = state[0]
        cs, acs = list(state[2]), list(state[3])
        start = pl.multiple_of(j * tk, tk)
        for pair in range(n_pairs):
            cols = slice(pair * LANES, (pair + 1) * LANES)
            kb = k_ref[0, pl.ds(start, tk), cols]
            vb = v_ref[0, pl.ds(start, tk), cols]
            probs = []
            for hd, qh in enumerate(q_heads(pair)):
                z = _dot_nt(qh, kb)
                log_keep, log_beta = _log_sigmoids(z)
                local, total = block_sums(log_keep)
                carry = cs[2 * pair + hd]
                probs.append(jnp.exp(log_beta + local + carry).astype(BF16))
                cs[2 * pair + hd] = carry + total
            acs[pair] = acs[pair] + _dot(jnp.concatenate(probs, axis=1), v_heads(vb))
        return j - 1, live(cs), tuple(cs), tuple(acs)

    state = lax.while_loop(cond, body, (w_start // tk - 1, live(carries), tuple(carries), tuple(accs)))
    for pair in range(n_pairs):
        o_ref[0, :, pair * LANES:(pair + 1) * LANES] = state[3][pair].astype(o_ref.dtype)


def _attn_call(q_arr, q_blk, k_arr, k_blk, v_arr, v_blk, sb_width, offset, tq):
    bsz, t_q, _ = q_arr.shape
    t_k = k_arr.shape[1]
    assert t_q % tq == 0 and offset % tq == 0 and (tq == ATTN_KEY_BLOCK or tq == t_q)
    assert t_k % ATTN_KEY_BLOCK == 0 and t_k >= ATTN_WINDOW and offset + t_q <= t_k
    kern = functools.partial(_attn_kernel, tq=tq, offset=offset)
    kv_spec = lambda blk: pl.BlockSpec((1, t_k, sb_width), lambda b, i: (b, 0, blk))
    return pl.pallas_call(
        kern, grid=(bsz, t_q // tq),
        in_specs=[pl.BlockSpec((1, tq, sb_width), lambda b, i: (b, i, q_blk)), kv_spec(k_blk), kv_spec(v_blk)],
        out_specs=pl.BlockSpec((1, tq, sb_width), lambda b, i: (b, i, 0)),
        out_shape=jax.ShapeDtypeStruct((bsz, t_q, sb_width), BF16),
        compiler_params=_params(("parallel", "arbitrary")), name="stickbreak",
    )(q_arr, k_arr, v_arr)


def _rg_kernel(xr_ref, xg_ref, cst_ref, h0_ref, cw_ref, cb_ref, wa_ref, ba_ref, wx_ref, bx_ref,
               lam_ref, rg_ref, cnew_ref, hlast_ref, hist_scr, hc_scr, *, tt):
    ti = pl.program_id(1)

    @pl.when(ti == 0)
    def _():
        hist_scr[...] = cst_ref[0]
        hc_scr[...] = h0_ref[0]

    x = xr_ref[0]
    conv_w = cw_ref.shape[0]
    xcat = jnp.concatenate([hist_scr[...], x], axis=0)
    xc = cb_ref[...] + cw_ref[conv_w - 1:conv_w, :] * x
    for k in range(1, conv_w):
        xc = xc + cw_ref[conv_w - 1 - k:conv_w - k, :] * pltpu.roll(xcat, k, axis=0)[SUBLANES:]
    hist_scr[...] = xcat[tt:]

    xcb = xc.astype(BF16)
    r = _sigmoid(_dot(xcb, wa_ref[...]) + ba_ref[...])
    ig = _sigmoid(_dot(xcb, wx_ref[...]) + bx_ref[...])
    log_a = (-RG_C * _softplus(-lam_ref[...])) * r
    a = jnp.exp(log_a)
    b = jnp.sqrt(_neg_expm1(2.0 * log_a)) * (ig * xc)

    row = lax.broadcasted_iota(jnp.int32, a.shape, 0)
    s = 1
    while s < tt:
        valid = row >= s
        b = b + a * jnp.where(valid, pltpu.roll(b, s, axis=0), 0.0)
        a = a * jnp.where(valid, pltpu.roll(a, s, axis=0), 1.0)
        s *= 2
    h = b + a * hc_scr[SUBLANES - 1:SUBLANES, :]
    hc_scr[...] = h[tt - SUBLANES:]
    rg_ref[0] = (h * _gelu_tanh(xg_ref[0])).astype(rg_ref.dtype)

    @pl.when(ti == pl.num_programs(1) - 1)
    def _():
        cnew_ref[0] = xcat[tt:]
        hlast_ref[0] = h[tt - SUBLANES:]


def _rg_call(xrg, conv_state8, h0_8, cw, cb, wa, ba, wx, bx, lam, tile=256):
    bsz, t_len, two_w = xrg.shape
    w = two_w // 2
    tt = _row_tile(t_len, tile)
    kern = functools.partial(_rg_kernel, tt=tt)
    state_spec = pl.BlockSpec((1, SUBLANES, w), lambda b, t: (b, 0, 0))
    state_shape = jax.ShapeDtypeStruct((bsz, SUBLANES, w), F32)
    return pl.pallas_call(
        kern, grid=(bsz, t_len // tt),
        in_specs=[pl.BlockSpec((1, tt, w), lambda b, t: (b, t, 0)),
                  pl.BlockSpec((1, tt, w), lambda b, t: (b, t, 1)),
                  state_spec, state_spec,
                  _const_spec(cw.shape), _const_spec(cb.shape), _const_spec(wa.shape), _const_spec(ba.shape),
                  _const_spec(wx.shape), _const_spec(bx.shape), _const_spec(lam.shape)],
        out_specs=[pl.BlockSpec((1, tt, w), lambda b, t: (b, t, 0)), state_spec, state_spec],
        out_shape=[jax.ShapeDtypeStruct((bsz, t_len, w), BF16), state_shape, state_shape],
        scratch_shapes=[pltpu.VMEM((SUBLANES, w), F32), pltpu.VMEM((SUBLANES, w), F32)],
        compiler_params=_params(("parallel", "arbitrary")), name="rglru",
    )(xrg, xrg, conv_state8, h0_8, cw, cb, wa, ba, wx, bx, lam)


def _convmod_glu(x, g_ref, w1_ref, b1_ref, ucat_scr, hist):
    d = x.shape[1]
    hn = _rms(x, g_ref[...]).astype(BF16)
    ag = _dot(hn, w1_ref[...]) + b1_ref[...]
    ucat_scr[hist:, :] = ag[:, :d] * _sigmoid(ag[:, d:])


def _conv_chunk(c, dw_ref, dwb_ref, ucat_scr, conv_scr, tt, hist, taps):
    cols = slice(c * LANES, (c + 1) * LANES)
    w_rows = [dw_ref[k:k + 1, cols] for k in range(taps)]
    bias = dwb_ref[:, cols]
    for r0 in range(0, tt, CONV_ROW_CHUNK):
        win = ucat_scr[r0:r0 + hist + CONV_ROW_CHUNK, cols]
        acc = bias + w_rows[taps - 1] * win[hist:]
        for r in range(SUBLANES):
            base = win if r == 0 else pltpu.roll(win, r, axis=0)
            for m in range((taps - 1 - r) // SUBLANES + 1):
                k = SUBLANES * m + r
                if k > 0:
                    lo = hist - SUBLANES * m
                    acc = acc + w_rows[taps - 1 - k] * base[lo:lo + CONV_ROW_CHUNK]
        conv_scr[r0:r0 + CONV_ROW_CHUNK, cols] = acc


def _convmod_out(x, lng_ref, lnb_ref, w2_ref, b2_ref, ucat_scr, conv_scr, tt, hist):
    new_hist = ucat_scr[tt:, :]
    ucat_scr[:hist, :] = new_hist
    cv = conv_scr[...]
    mu = jnp.mean(cv, axis=-1, keepdims=True)
    xc = cv - mu
    ln = xc * lax.rsqrt(jnp.mean(xc * xc, axis=-1, keepdims=True) + NORM_EPS) * lng_ref[...] + lnb_ref[...]
    act = (ln * _sigmoid(ln)).astype(BF16)
    return x + _dot(act, w2_ref[...]) + b2_ref[...], new_hist


def _convmod_tile(x, g_ref, w1_ref, b1_ref, dw_ref, dwb_ref, lng_ref, lnb_ref, w2_ref, b2_ref,
                  ucat_scr, conv_scr, *, tt, hist, taps):
    _convmod_glu(x, g_ref, w1_ref, b1_ref, ucat_scr, hist)
    for c in range(x.shape[1] // LANES):
        _conv_chunk(c, dw_ref, dwb_ref, ucat_scr, conv_scr, tt, hist, taps)
    return _convmod_out(x, lng_ref, lnb_ref, w2_ref, b2_ref, ucat_scr, conv_scr, tt, hist)


def _convmod_kernel(h_ref, cst_ref, *refs, tt, hist, taps):
    consts, (o_ref, cnew_ref, ucat_scr, conv_scr) = refs[:9], refs[9:]
    ti = pl.program_id(1)

    @pl.when(ti == 0)
    def _():
        ucat_scr[:hist, :] = cst_ref[0]

    o_ref[0], new_hist = _convmod_tile(h_ref[0], *consts, ucat_scr, conv_scr, tt=tt, hist=hist, taps=taps)

    @pl.when(ti == pl.num_programs(1) - 1)
    def _():
        cnew_ref[0] = new_hist


def _convmod_call(h, conv_state_pad, g, w1, b1, dw, dwb, lng, lnb, w2, b2, tile=256):
    bsz, t_len, d = h.shape
    hist = conv_state_pad.shape[1]
    taps = dw.shape[0]
    tt = _row_tile(t_len, tile)
    assert tt >= hist and hist % SUBLANES == 0 and taps - 1 <= hist
    kern = functools.partial(_convmod_kernel, tt=tt, hist=hist, taps=taps)
    state_spec = pl.BlockSpec((1, hist, d), lambda b, t: (b, 0, 0))
    consts = [g, w1, b1, dw, dwb, lng, lnb, w2, b2]
    return pl.pallas_call(
        kern, grid=(bsz, t_len // tt),
        in_specs=[pl.BlockSpec((1, tt, d), lambda b, t: (b, t, 0)), state_spec]
                 + [_const_spec(c.shape) for c in consts],
        out_specs=[pl.BlockSpec((1, tt, d), lambda b, t: (b, t, 0)), state_spec],
        out_shape=[jax.ShapeDtypeStruct((bsz, t_len, d), F32),
                   jax.ShapeDtypeStruct((bsz, hist, d), F32)],
        scratch_shapes=[pltpu.VMEM((hist + tt, d), F32), pltpu.VMEM((tt, d), F32)],
        compiler_params=_params(("parallel", "arbitrary")), name="convmod",
    )(h, conv_state_pad, *consts)


def _convffn_kernel(h_ref, cst_ref, *refs, tt, hist, taps, tiles_per_seq, has_final, ff_splits):
    cm_consts, refs = refs[:9], refs[9:]
    g_ref, wg_ref, wu_ref, wd_ref = refs[:4]
    ple = refs[4:8]
    gf_ref = refs[8] if has_final else None
    o_ref, cnew_ref, ucat_scr, conv_scr, h_scr = refs[8 + has_final:]
    s = pl.program_id(0)
    n_tiles = pl.num_programs(0) - 1
    ti = lax.rem(jnp.minimum(s, n_tiles - 1), tiles_per_seq)

    @pl.when(s == 0)
    def _():
        h_scr[...] = jnp.zeros_like(h_scr)

    @pl.when(ti == 0)
    def _():
        ucat_scr[:hist, :] = cst_ref[0]

    g_cm, w1_ref, b1_ref, dw_ref, dwb_ref, lng_ref, lnb_ref, w2_ref, b2_ref = cm_consts
    h_prev = h_scr[...]
    x = h_ref[...]
    _convmod_glu(x, g_cm, w1_ref, b1_ref, ucat_scr, hist)
    conv_chunks = [functools.partial(_conv_chunk, c, dw_ref, dwb_ref, ucat_scr, conv_scr, tt, hist, taps)
                   for c in range(x.shape[1] // LANES)]
    o_ref[...] = _ffn_tile(h_prev, g_ref, wg_ref, wu_ref, wd_ref, ple, gf_ref, ff_splits, side_work=conv_chunks)
    h_new, new_hist = _convmod_out(x, lng_ref, lnb_ref, w2_ref, b2_ref, ucat_scr, conv_scr, tt, hist)
    h_scr[...] = h_new

    @pl.when(jnp.logical_and(ti == tiles_per_seq - 1, s < n_tiles))
    def _():
        cnew_ref[0] = new_hist


def _convffn_call(h, t_len, conv_state_pad, cm_consts, g, wg, wu, wd, ple, final_g, tile):
    n, d = h.shape
    hist = conv_state_pad.shape[1]
    taps = cm_consts[3].shape[0]
    tt = tile
    assert t_len % tt == 0 and tt >= hist and hist % SUBLANES == 0 and taps - 1 <= hist
    tiles_per_seq, n_tiles = t_len // tt, n // tt
    p, gp, wpg, wpp = ple
    conv_tile = lambda s: jnp.minimum(s, n_tiles - 1)
    ffn_tile = lambda s: jnp.maximum(s - 1, 0)
    state_spec = pl.BlockSpec((1, hist, d), lambda s: (conv_tile(s) // tiles_per_seq, 0, 0))
    consts = list(cm_consts) + [g, wg, wu, wd]
    tail = [gp, wpg, wpp] + ([final_g] if final_g is not None else [])
    kern = functools.partial(_convffn_kernel, tt=tt, hist=hist, taps=taps, tiles_per_seq=tiles_per_seq,
                             has_final=final_g is not None, ff_splits=_ff_splits(wg.shape[1], 4))
    return pl.pallas_call(
        kern, grid=(n_tiles + 1,),
        in_specs=[pl.BlockSpec((tt, d), lambda s: (conv_tile(s), 0)), state_spec]
                 + [_const_spec(c.shape) for c in consts]
                 + [pl.BlockSpec((tt, p.shape[1]), lambda s: (ffn_tile(s), 0))]
                 + [_const_spec(c.shape) for c in tail],
        out_specs=[pl.BlockSpec((tt, d), lambda s: (ffn_tile(s), 0)), state_spec],
        out_shape=[jax.ShapeDtypeStruct((n, d), F32),
                   jax.ShapeDtypeStruct((conv_state_pad.shape[0], hist, d), F32)],
        scratch_shapes=[pltpu.VMEM((hist + tt, d), F32), pltpu.VMEM((tt, d), F32), pltpu.VMEM((tt, d), F32)],
        compiler_params=_params(("arbitrary",)), name="convffn",
    )(h, conv_state_pad, *consts, p, *tail)


def _block_diag(w):
    g, i, j = w.shape
    eye = jnp.eye(g, dtype=w.dtype)
    return (eye[:, None, :, None] * w[:, :, None, :]).reshape(g * i, g * j)


def _pad_rows_front(x, rows):
    return jnp.pad(x, ((0, 0), (rows - x.shape[1], 0), (0, 0)))


def _trunk(x, p, w, sb_k, sb_v, rg_conv, rg_h, cm_conv):
    bsz, t_len, d = x.shape
    n = bsz * t_len
    fresh = sb_k is None
    depth = w['norm_ffn'].shape[0]
    row2 = lambda v: v.reshape(1, -1)
    bf = lambda v: v.astype(BF16)
    h = x.reshape(n, d)
    out_k, out_v, out_rc, out_rh, out_cc = [], [], [], [], []
    for i in range(depth):
        j = i // 2
        ffn = lambda hh, s, **kw: _ffn_call(
            hh, row2(w['norm_ffn'][i, s]), bf(w['w_ffn_gate'][i, s]), bf(w['w_ffn_up'][i, s]),
            bf(w['w_ffn_down'][i, s]), **kw)
        ple = (p[i].reshape(n, -1), row2(w['norm_ple'][i]), bf(w['w_ple_gate'][i]), bf(w['w_ple_proj'][i]))
        final_g = row2(w['norm_final']) if i == depth - 1 else None
        h = ffn(h, 0)
        if i % 2 == 0:
            rg_w = w['rg_lambda'].shape[1]
            sb_w = (w['w_ab_in'].shape[2] - 2 * rg_w) // 3
            qkv, k_new, v_new, xrg = _inproj_call(h, row2(w['norm_mix'][i]), bf(w['w_ab_in'][j]),
                                                  bsz, t_len, sb_w)
            qkv3 = qkv.reshape(bsz, t_len, 3 * sb_w)
            if fresh:
                sb = _attn_call(qkv3, 0, qkv3, 1, qkv3, 2, sb_w, 0, ATTN_KEY_BLOCK)
                conv0 = jnp.zeros((bsz, SUBLANES, rg_w), F32)
                h0 = jnp.zeros((bsz, SUBLANES, rg_w), F32)
            else:
                past = sb_k.shape[3]
                t_k = -(-(past + t_len) // ATTN_KEY_BLOCK) * ATTN_KEY_BLOCK

                def with_cache(cache, blk):
                    tm_major = cache.transpose(0, 2, 1, 3).reshape(bsz, past, sb_w).astype(BF16)
                    new = qkv3[:, :, blk * sb_w:(blk + 1) * sb_w]
                    full = jnp.concatenate([tm_major, new], axis=1)
                    return jnp.pad(full, ((0, 0), (0, t_k - past - t_len), (0, 0)))

                sb = _attn_call(qkv3, 0, with_cache(sb_k[j], 1), 0, with_cache(sb_v[j], 2), 0,
                                sb_w, past, t_len)
                conv0 = _pad_rows_front(rg_conv[j], SUBLANES)
                h0 = jnp.broadcast_to(rg_h[j][:, None, :], (bsz, SUBLANES, rg_w))
            rg, rc8, rh8 = _rg_call(
                xrg.reshape(bsz, t_len, 2 * rg_w), conv0, h0, w['rg_conv_w'][j], row2(w['rg_conv_b'][j]),
                bf(_block_diag(w['rg_gate_a_w'][j])), row2(w['rg_gate_a_b'][j]),
                bf(_block_diag(w['rg_gate_x_w'][j])), row2(w['rg_gate_x_b'][j]), row2(w['rg_lambda'][j]))
            out_k.append(k_new)
            out_v.append(v_new)
            out_rc.append(rc8[:, SUBLANES - (w['rg_conv_w'].shape[1] - 1):])
            out_rh.append(rh8[:, SUBLANES - 1])
            mix = (sb.reshape(n, sb_w), rg.reshape(n, rg_w), bf(w['w_ab_out'][j]))
            h = ffn(h, 1, mix=mix, ple=ple, final_g=final_g)
        else:
            taps = w['cm_dw_w'].shape[1]
            hist = -(-(taps - 1) // SUBLANES) * SUBLANES
            cc0 = jnp.zeros((bsz, hist, d), F32) if fresh else _pad_rows_front(cm_conv[j], hist)
            cm_consts = (row2(w['norm_mix'][i]), bf(w['cm_pw1_w'][j]), row2(w['cm_pw1_b'][j]), w['cm_dw_w'][j],
                         row2(w['cm_dw_b'][j]), row2(w['cm_ln_g'][j]), row2(w['cm_ln_b'][j]),
                         bf(w['cm_pw2_w'][j]), row2(w['cm_pw2_b'][j]))
            if t_len >= FUSED_TILE:
                h, cc_new = _convffn_call(
                    h, t_len, cc0, cm_consts, row2(w['norm_ffn'][i, 1]), bf(w['w_ffn_gate'][i, 1]),
                    bf(w['w_ffn_up'][i, 1]), bf(w['w_ffn_down'][i, 1]), ple, final_g, FUSED_TILE)
            else:
                h3, cc_new = _convmod_call(h.reshape(bsz, t_len, d), cc0, *cm_consts)
                h = ffn(h3.reshape(n, d), 1, ple=ple, final_g=final_g)
            out_cc.append(cc_new[:, hist - (taps - 1):])
    y = h.reshape(bsz, t_len, d)
    return y, jnp.stack(out_k), jnp.stack(out_v), jnp.stack(out_rc), jnp.stack(out_rh), jnp.stack(out_cc)


def kernel(x_prompt, x_sample, cache_sb_k, cache_sb_v, state_rg_conv, state_rg_h, state_cm_conv,
           p_prompt, p_sample, norm_ffn, w_ffn_gate, w_ffn_up, w_ffn_down, norm_mix, w_ab_in, w_ab_out,
           rg_conv_w, rg_conv_b, rg_gate_a_w, rg_gate_a_b, rg_gate_x_w, rg_gate_x_b, rg_lambda,
           cm_pw1_w, cm_pw1_b, cm_dw_w, cm_dw_b, cm_ln_g, cm_ln_b, cm_pw2_w, cm_pw2_b,
           norm_ple, w_ple_gate, w_ple_proj, norm_final):
    w = dict(norm_ffn=norm_ffn, w_ffn_gate=w_ffn_gate, w_ffn_up=w_ffn_up, w_ffn_down=w_ffn_down,
             norm_mix=norm_mix, w_ab_in=w_ab_in, w_ab_out=w_ab_out, rg_conv_w=rg_conv_w, rg_conv_b=rg_conv_b,
             rg_gate_a_w=rg_gate_a_w, rg_gate_a_b=rg_gate_a_b, rg_gate_x_w=rg_gate_x_w, rg_gate_x_b=rg_gate_x_b,
             rg_lambda=rg_lambda, cm_pw1_w=cm_pw1_w, cm_pw1_b=cm_pw1_b, cm_dw_w=cm_dw_w, cm_dw_b=cm_dw_b,
             cm_ln_g=cm_ln_g, cm_ln_b=cm_ln_b, cm_pw2_w=cm_pw2_w, cm_pw2_b=cm_pw2_b,
             norm_ple=norm_ple, w_ple_gate=w_ple_gate, w_ple_proj=w_ple_proj, norm_final=norm_final)
    y_p, k_p, v_p, rc_p, rh_p, cc_p = _trunk(x_prompt, p_prompt, w, None, None, None, None, None)
    y_s, k_s, v_s, rc_s, rh_s, cc_s = _trunk(x_sample, p_sample, w, cache_sb_k, cache_sb_v,
                                             state_rg_conv, state_rg_h, state_cm_conv)
    return (y_p, y_s, k_p, v_p, rc_p, rh_p, cc_p, k_s, v_s, rc_s, rh_s, cc_s)
```
